```python
import math
import jax, jax.numpy as jnp
from jax import lax
import numpy as np

D_MODEL = 2048
BATCH = 2
SEQ = 4096
DEPTH = 4
DEC_BATCH = 8
DEC_SEQ = 2048
PAST_LEN = 128

N_META = 16
S5_WIDTH = 512
S5_GROUP = 16
S5_GROUPS = S5_WIDTH // S5_GROUP
S5_STATE = 64
GLA_HEADS = 4
GLA_DK = 64
GLA_DV = 128
GLA_KEY = GLA_HEADS * GLA_DK
GLA_WIDTH = GLA_HEADS * GLA_DV
GLA_RANK = 16
GLA_GATE_NORM = 16.0
GLA_CHUNK = 64
LRU_WIDTH = 1024
LRU_BLOCKS = 8
LRU_BLOCK = LRU_WIDTH // LRU_BLOCKS
CONV_WIDTH = 4
LRU_C = 8.0
N_BRANCH = 3
EPS = 1e-6
N_IN = 2 * S5_WIDTH + 2 * GLA_KEY + 2 * GLA_WIDTH + 2 * GLA_RANK + 2 * LRU_WIDTH + N_BRANCH * D_MODEL

kernel_name = 'hybrid_s5_gla_rglru_parallel_encoder'

F32 = jnp.float32


def rmsnorm(x, g):
    xf = x.astype(F32)
    return xf * lax.rsqrt(jnp.mean(xf * xf, axis=-1, keepdims=True) + EPS) * g.astype(F32)


def linear_scan(a, b, axis, reverse=False):
    if reverse:
        a = jnp.flip(a, axis)
        b = jnp.flip(b, axis)
    def combine(l, r):
        return r[0] * l[0], r[0] * l[1] + r[1]
    _, h = lax.associative_scan(combine, (a, b), axis=axis)
    if reverse:
        h = jnp.flip(h, axis)
    return h


def complex_linear_scan(a_re, a_im, b_re, b_im, axis, reverse=False):
    elems = (a_re, a_im, b_re, b_im)
    if reverse:
        elems = tuple(jnp.flip(e, axis) for e in elems)
    def combine(l, r):
        alr, ali, blr, bli = l
        arr, ari, brr, bri = r
        return (arr * alr - ari * ali, arr * ali + ari * alr,
                arr * blr - ari * bli + brr, arr * bli + ari * blr + bri)
    _, _, h_re, h_im = lax.associative_scan(combine, elems, axis=axis)
    if reverse:
        h_re = jnp.flip(h_re, axis)
        h_im = jnp.flip(h_im, axis)
    return h_re, h_im


def split_cols(c):
    sizes = (S5_WIDTH, S5_WIDTH, GLA_KEY, GLA_KEY, GLA_WIDTH, GLA_WIDTH, 2 * GLA_RANK,
             LRU_WIDTH, LRU_WIDTH, N_BRANCH * D_MODEL)
    outs = []
    start = 0
    for s in sizes:
        outs.append(c[..., start:start + s])
        start += s
    return outs


def s5_mixer(u, lam_re, lam_im, log_step, b_re, b_im, c_re, c_im, d_skip, w_glu, b_glu):
    bsz, length, _ = u.shape
    ug = u.reshape(bsz, length, S5_GROUPS, S5_GROUP)
    b_re = b_re.astype(F32)
    b_im = b_im.astype(F32)
    y = d_skip.astype(F32) * u
    for d in range(2):
        lr = lam_re[d].astype(F32)
        li = lam_im[d].astype(F32)
        dt = jnp.exp(log_step[d].astype(F32))[:, None]
        mag = jnp.exp(lr * dt)
        abr = mag * jnp.cos(li * dt)
        abi = mag * jnp.sin(li * dt)
        den = lr * lr + li * li
        fr = ((abr - 1.0) * lr + abi * li) / den
        fi = (abi * lr - (abr - 1.0) * li) / den
        bbr = fr[..., None] * b_re - fi[..., None] * b_im
        bbi = fr[..., None] * b_im + fi[..., None] * b_re
        bur = jnp.einsum('gnc,blgc->blgn', bbr, ug)
        bui = jnp.einsum('gnc,blgc->blgn', bbi, ug)
        s_re, s_im = complex_linear_scan(jnp.broadcast_to(abr, bur.shape), jnp.broadcast_to(abi, bur.shape),
                                         bur, bui, axis=1, reverse=(d == 1))
        yg = (jnp.einsum('gcn,blgn->blgc', c_re[d].astype(F32), s_re)
              - jnp.einsum('gcn,blgn->blgc', c_im[d].astype(F32), s_im))
        y = y + yg.reshape(bsz, length, S5_WIDTH)
    z = jax.nn.gelu(y)
    return z * jax.nn.sigmoid(z @ w_glu + b_glu)


def gla_chunked(q, k, v, g):
    bsz, t_len, heads, dk = q.shape
    dv = v.shape[-1]
    n_chunks = t_len // GLA_CHUNK
    rs = lambda t: t.reshape(bsz, n_chunks, GLA_CHUNK, heads, t.shape[-1]).astype(F32)
    q, k, v, g = rs(q), rs(k), rs(v), rs(g)
    b = jnp.cumsum(g, axis=2)
    qe = q * jnp.exp(b)
    ke = k * jnp.exp(-b)
    mask = jnp.tril(jnp.ones((GLA_CHUNK, GLA_CHUNK), dtype=bool))
    att = jnp.where(mask, jnp.einsum('bnihk,bnjhk->bnhij', qe, ke), 0.0)
    o = jnp.einsum('bnhij,bnjhv->bnihv', att, v)
    b_last = b[:, :, -1:]
    ds = jnp.einsum('bnjhk,bnjhv->bnhkv', k * jnp.exp(b_last - b), v)
    decay = jnp.broadcast_to(jnp.exp(b_last[:, :, 0])[..., None], ds.shape)
    s = linear_scan(decay, ds, axis=1)
    s_prev = jnp.concatenate([jnp.zeros_like(s[:, :1]), s[:, :-1]], axis=1)
    o = o + jnp.einsum('bnihk,bnhkv->bnihv', qe, s_prev)
    return o.reshape(bsz, t_len, heads, dv)


def gla_mixer(q, k, v, glr, w_gate_up, b_gate, norm_g):
    bsz, length, _ = q.shape
    q = q.reshape(bsz, length, GLA_HEADS, GLA_DK) * (GLA_DK ** -0.5)
    k = k.reshape(bsz, length, GLA_HEADS, GLA_DK)
    v = v.reshape(bsz, length, GLA_HEADS, GLA_DV)
    pad = (-N_META) % GLA_CHUNK
    padt = lambda t: jnp.pad(t, ((0, 0), (pad, 0), (0, 0), (0, 0)))
    o = 0.0
    for d in range(2):
        g = jax.nn.log_sigmoid(glr[..., d * GLA_RANK:(d + 1) * GLA_RANK] @ w_gate_up[d] + b_gate[d]) / GLA_GATE_NORM
        g = g.reshape(bsz, length, GLA_HEADS, GLA_DK)
        qp, kp, vp, gp = padt(q), padt(k), padt(v), padt(g)
        if d == 1:
            qp, kp, vp, gp = (jnp.flip(t, 1) for t in (qp, kp, vp, gp))
        od = gla_chunked(qp, kp, vp, gp)
        if d == 1:
            od = jnp.flip(od, 1)
        o = o + od[:, pad:]
    o = o * lax.rsqrt(jnp.mean(o * o, axis=-1, keepdims=True) + EPS)
    o = o * norm_g.astype(F32).reshape(GLA_HEADS, GLA_DV)
    return o.reshape(bsz, length, GLA_WIDTH)


def rglru_mixer(x, conv_w, conv_b, w_a, b_a, w_x, b_x, lam):
    bsz, length, _ = x.shape
    left = CONV_WIDTH // 2
    xp = jnp.pad(x, ((0, 0), (left, CONV_WIDTH - 1 - left), (0, 0)))
    xc = conv_b.astype(F32)
    for j in range(CONV_WIDTH):
        xc = xc + xp[:, j:j + length] * conv_w[j]
    xb = xc.reshape(bsz, length, LRU_BLOCKS, LRU_BLOCK)
    h = 0.0
    for d in range(2):
        r = jax.nn.sigmoid(jnp.einsum('blnc,ncd->blnd', xb, w_a[d]).reshape(bsz, length, LRU_WIDTH) + b_a[d])
        i = jax.nn.sigmoid(jnp.einsum('blnc,ncd->blnd', xb, w_x[d]).reshape(bsz, length, LRU_WIDTH) + b_x[d])
        log_a = -LRU_C * r * jax.nn.softplus(-lam[d].astype(F32))
        a = jnp.exp(log_a)
        bt = jnp.sqrt(-jnp.expm1(2.0 * log_a)) * (i * xc)
        h = h + linear_scan(a, bt, axis=1, reverse=(d == 1))
    return h


def mixer_layer(z, norm_g, w_in, s5_lam_re, s5_lam_im, s5_log_step, s5_b_re, s5_b_im, s5_c_re, s5_c_im,
                s5_d, s5_w_glu, s5_b_glu, gla_w_gate_up, gla_b_gate, gla_norm_g, conv_w, conv_b,
                lru_w_a, lru_b_a, lru_w_x, lru_b_x, lru_lam, w_out_a, w_out_b, w_out_c, w_o):
    bsz, length, _ = z.shape
    h = rmsnorm(z, norm_g)
    cols = h @ w_in
    u_a, gate_a, q_b, k_b, v_b, gate_b, glr_b, x_c, gate_c, merge = split_cols(cols)
    y_a = s5_mixer(u_a, s5_lam_re, s5_lam_im, s5_log_step, s5_b_re, s5_b_im, s5_c_re, s5_c_im,
                   s5_d, s5_w_glu, s5_b_glu) * jax.nn.silu(gate_a)
    y_b = gla_mixer(q_b, k_b, v_b, glr_b, gla_w_gate_up, gla_b_gate, gla_norm_g) * jax.nn.silu(gate_b)
    y_c = rglru_mixer(x_c, conv_w, conv_b, lru_w_a, lru_b_a, lru_w_x, lru_b_x, lru_lam) * jax.nn.silu(gate_c)
    gates = jax.nn.sigmoid(merge).reshape(bsz, length, N_BRANCH, D_MODEL)
    m = (gates[:, :, 0] * (y_a @ w_out_a) + gates[:, :, 1] * (y_b @ w_out_b)
         + gates[:, :, 2] * (y_c @ w_out_c))
    return z + m @ w_o


def encoder(x, meta_tokens, layer_params, final_norm_g):
    bsz = x.shape[0]
    meta = jnp.broadcast_to(meta_tokens.astype(F32)[None], (bsz, N_META, D_MODEL))
    z = jnp.concatenate([meta, x.astype(F32)], axis=1)
    for l in range(DEPTH):
        z = mixer_layer(z, *[p[l] for p in layer_params])
    return rmsnorm(z, final_norm_g)[:, N_META:].astype(x.dtype)


def setup_inputs(seed: int = 0) -> dict:
    key = jax.random.key(seed)
    ks = jax.random.split(key, 40)
    nrm = lambda k, shape, scale: jax.random.normal(k, shape, F32) * scale
    lam_im_base = math.pi * jnp.arange(S5_STATE, dtype=F32)
    u_lru = jax.random.uniform(ks[24], (DEPTH, 2, LRU_WIDTH), F32, minval=0.9, maxval=0.999)
    a_base = u_lru ** (1.0 / LRU_C)
    return {
        'x_prompt': nrm(ks[0], (BATCH, SEQ, D_MODEL), 1.0),
        'x_sample': nrm(ks[1], (DEC_BATCH, DEC_SEQ, D_MODEL), 1.0),
        'meta_tokens': nrm(ks[2], (N_META, D_MODEL), 1.0),
        'norm_g': 1.0 + nrm(ks[3], (DEPTH, D_MODEL), 0.01),
        'w_in': nrm(ks[4], (DEPTH, D_MODEL, N_IN), D_MODEL ** -0.5),
        's5_lam_re': -0.5 + nrm(ks[5], (DEPTH, 2, S5_GROUPS, S5_STATE), 0.01),
        's5_lam_im': lam_im_base + nrm(ks[6], (DEPTH, 2, S5_GROUPS, S5_STATE), 0.01),
        's5_log_step': jax.random.uniform(ks[7], (DEPTH, 2, S5_GROUPS), F32,
                                          minval=math.log(1e-3), maxval=math.log(1e-1)),
        's5_b_re': nrm(ks[8], (DEPTH, S5_GROUPS, S5_STATE, S5_GROUP), (2 * S5_GROUP) ** -0.5),
        's5_b_im': nrm(ks[9], (DEPTH, S5_GROUPS, S5_STATE, S5_GROUP), (2 * S5_GROUP) ** -0.5),
        's5_c_re': nrm(ks[10], (DEPTH, 2, S5_GROUPS, S5_GROUP, S5_STATE), (2 * S5_STATE) ** -0.5),
        's5_c_im': nrm(ks[11], (DEPTH, 2, S5_GROUPS, S5_GROUP, S5_STATE), (2 * S5_STATE) ** -0.5),
        's5_d': nrm(ks[12], (DEPTH, S5_WIDTH), 1.0),
        's5_w_glu': nrm(ks[13], (DEPTH, S5_WIDTH, S5_WIDTH), S5_WIDTH ** -0.5),
        's5_b_glu': nrm(ks[14], (DEPTH, S5_WIDTH), 0.01),
        'gla_w_gate_up': nrm(ks[15], (DEPTH, 2, GLA_RANK, GLA_KEY), GLA_RANK ** -0.5),
        'gla_b_gate': nrm(ks[16], (DEPTH, 2, GLA_KEY), 0.01),
        'gla_norm_g': 1.0 + nrm(ks[17], (DEPTH, GLA_WIDTH), 0.01),
        'conv_w': nrm(ks[18], (DEPTH, CONV_WIDTH, LRU_WIDTH), CONV_WIDTH ** -0.5),
        'conv_b': nrm(ks[19], (DEPTH, LRU_WIDTH), 0.01),
        'lru_w_a': nrm(ks[20], (DEPTH, 2, LRU_BLOCKS, LRU_BLOCK, LRU_BLOCK), LRU_BLOCK ** -0.5),
        'lru_b_a': nrm(ks[21], (DEPTH, 2, LRU_WIDTH), 0.01),
        'lru_w_x': nrm(ks[22], (DEPTH, 2, LRU_BLOCKS, LRU_BLOCK, LRU_BLOCK), LRU_BLOCK ** -0.5),
        'lru_b_x': nrm(ks[23], (DEPTH, 2, LRU_WIDTH), 0.01),
        'lru_lam': jnp.log(a_base) - jnp.log1p(-a_base),
        'w_out_a': nrm(ks[25], (DEPTH, S5_WIDTH, D_MODEL), S5_WIDTH ** -0.5),
        'w_out_b': nrm(ks[26], (DEPTH, GLA_WIDTH, D_MODEL), GLA_WIDTH ** -0.5),
        'w_out_c': nrm(ks[27], (DEPTH, LRU_WIDTH, D_MODEL), LRU_WIDTH ** -0.5),
        'w_o': nrm(ks[28], (DEPTH, D_MODEL, D_MODEL), D_MODEL ** -0.5),
        'final_norm_g': 1.0 + nrm(ks[29], (D_MODEL,), 0.01),
    }


def reference(x_prompt, x_sample, meta_tokens, norm_g, w_in, s5_lam_re, s5_lam_im, s5_log_step,
              s5_b_re, s5_b_im, s5_c_re, s5_c_im, s5_d, s5_w_glu, s5_b_glu, gla_w_gate_up, gla_b_gate,
              gla_norm_g, conv_w, conv_b, lru_w_a, lru_b_a, lru_w_x, lru_b_x, lru_lam,
              w_out_a, w_out_b, w_out_c, w_o, final_norm_g):
    layer_params = (norm_g, w_in, s5_lam_re, s5_lam_im, s5_log_step, s5_b_re, s5_b_im, s5_c_re, s5_c_im,
                    s5_d, s5_w_glu, s5_b_glu, gla_w_gate_up, gla_b_gate, gla_norm_g, conv_w, conv_b,
                    lru_w_a, lru_b_a, lru_w_x, lru_b_x, lru_lam, w_out_a, w_out_b, w_out_c, w_o)
    y_prompt = encoder(x_prompt, meta_tokens, layer_params, final_norm_g)
    y_sample = encoder(x_sample, meta_tokens, layer_params, final_norm_g)
    return (y_prompt, y_sample)
```

```python
import functools

import jax
import jax.numpy as jnp
from jax import lax
from jax.experimental import pallas as pl
from jax.experimental.pallas import tpu as pltpu

F32 = jnp.float32
BF16 = jnp.bfloat16
HIGHEST = lax.Precision.HIGHEST

N_META = 16
PAD = 48
EPS = 1e-6
S5_WIDTH = 512
S5_T = 16
S5_GB = 8
S5_NS = 512
GLA_HEADS = 4
GLA_DK = 64
GLA_DV = 128
GLA_KEY = 256
GLA_WIDTH = 512
GLA_RANK = 16
GLA_GATE_NORM = 16.0
GLA_CHUNK = 64
LRU_WIDTH = 1024
LRU_BLOCKS = 8
LRU_BLOCK = 128
CONV_WIDTH = 4
LRU_C = 8.0
HALO = 8
VMEM_LIMIT = 56 * 1024 * 1024

C1_X, C1_GC, C1_U, C1_GA, C1_V, C1_GB = 0, 1024, 2048, 2560, 3072, 3584
N1 = 4096
N2 = 640
TN = 512
NJ1 = N1 // TN
TJ = 256


def _dot(a, b):
    return jnp.dot(a, b, preferred_element_type=F32)


def _dot_nt(a, b):
    return lax.dot_general(a, b, (((1,), (1,)), ((), ())), preferred_element_type=F32)


def _dot_nt_hi(a, b):
    return lax.dot_general(a, b, (((1,), (1,)), ((), ())), precision=HIGHEST, preferred_element_type=F32)


def _dot_tn(a, b):
    return lax.dot_general(a, b, (((0,), (0,)), ((), ())), preferred_element_type=F32)


def _params(sem):
    return pltpu.CompilerParams(dimension_semantics=sem, vmem_limit_bytes=VMEM_LIMIT)


def _kin_kernel(z_ref, g_ref, w1_ref, w2_ref, o1_ref, o2_ref, h_ref):
    j = pl.program_id(1)

    @pl.when(j == 0)
    def _():
        z = z_ref[...]
        ms = jnp.mean(z * z, axis=-1, keepdims=True)
        h_ref[...] = (z * lax.rsqrt(ms + EPS) * g_ref[...]).astype(BF16)

    @pl.when(j < NJ1)
    def _():
        o1_ref[...] = _dot(h_ref[...], w1_ref[...])

    @pl.when(j == NJ1)
    def _():
        o2_ref[...] = _dot(h_ref[...], w2_ref[...])


def _in_proj(z, g, w1, w2, tm):
    r, d = z.shape
    last = NJ1 - 1
    return pl.pallas_call(
        _kin_kernel,
        grid=(r // tm, NJ1 + 1),
        in_specs=[
            pl.BlockSpec((tm, d), lambda i, j: (i, 0)),
            pl.BlockSpec((1, d), lambda i, j: (0, 0)),
            pl.BlockSpec((d, TN), lambda i, j: (0, jnp.minimum(j, last))),
            pl.BlockSpec((d, N2), lambda i, j: (0, 0)),
        ],
        out_specs=[
            pl.BlockSpec((tm, TN), lambda i, j: (i, jnp.minimum(j, last))),
            pl.BlockSpec((tm, N2), lambda i, j: (i, 0)),
        ],
        out_shape=[jax.ShapeDtypeStruct((r, N1), F32), jax.ShapeDtypeStruct((r, N2), F32)],
        scratch_shapes=[pltpu.VMEM((tm, d), BF16)],
        compiler_params=_params(("parallel", "arbitrary")),
        name="in_proj",
    )(z, g, w1, w2)


def _s5_kernel(u_ref, prm_ref, lbr_ref, lbi_ref, ctr_ref, cti_ref, dsk_ref, o_ref,
               wst_ref, m_ref, wout_ref, e_ref, w_ref, s_ref, a16_ref, *, nc):
    b = pl.program_id(1)
    ns = S5_NS
    blk = S5_GB * 16

    @pl.when(b == 0)
    def _build():
        prm = prm_ref[0, 0]
        evec = lax.broadcasted_iota(jnp.int32, (24, 1), 0).astype(F32)
        e_ref[...] = jnp.zeros_like(e_ref)
        for d in range(2):
            lr = prm[3 * d:3 * d + 1]
            li = prm[3 * d + 1:3 * d + 2]
            dt = jnp.exp(prm[3 * d + 2:3 * d + 3])
            mag = jnp.exp(evec * (lr * dt))
            ang = evec * (li * dt)
            pre = mag * jnp.cos(ang)
            pim = mag * jnp.sin(ang)
            abr = pre[1:2]
            abi = pim[1:2]
            den = lr * lr + li * li
            fr = ((abr - 1.0) * lr + abi * li) / den
            fi = (abi * lr - (abr - 1.0) * li) / den
            lbr = lbr_ref[0, 0]
            lbi = lbi_ref[0, 0]
            bbr = fr * lbr - fi * lbi
            bbi = fr * lbi + fi * lbr
            ctr = ctr_ref[0, d, 0]
            cti = cti_ref[0, d, 0]
            a16_ref[2 * d:2 * d + 1, :] = pre[S5_T:S5_T + 1]
            a16_ref[2 * d + 1:2 * d + 2, :] = pim[S5_T:S5_T + 1]
            for r in range(S5_T):
                e = S5_T - 1 - r if d == 0 else r
                pr_e = pre[e:e + 1]
                pi_e = pim[e:e + 1]
                l_re = bbr * pr_e - bbi * pi_e
                l_im = bbr * pi_e + bbi * pr_e
                base = 2 * d * ns
                wst_ref[r * blk:(r + 1) * blk, base:base + ns] = l_re.astype(BF16)
                wst_ref[r * blk:(r + 1) * blk, base + ns:base + 2 * ns] = l_im.astype(BF16)
                kblk = _dot_nt_hi(l_re, ctr) - _dot_nt_hi(l_im, cti)
                eb = r if d == 0 else S5_T - 1 + r
                e_ref[eb * blk:(eb + 1) * blk, :] += kblk
                eo = r + 1 if d == 0 else S5_T - r
                pr_o = pre[eo:eo + 1]
                pi_o = pim[eo:eo + 1]
                wout_ref[r * blk:(r + 1) * blk, base:base + ns] = (pr_o * ctr - pi_o * cti).astype(BF16)
                wout_ref[r * blk:(r + 1) * blk, base + ns:base + 2 * ns] = (
                    -(pi_o * ctr) - pr_o * cti).astype(BF16)
        for t in range(S5_T):
            off = (S5_T - 1 - t) * blk
            m_ref[:, t * blk:(t + 1) * blk] = e_ref[off:off + S5_T * blk, :].astype(BF16)

    x = jnp.concatenate([u_ref[0, pl.ds(t, nc, stride=S5_T), :] for t in range(S5_T)], axis=1)
    xb = x.astype(BF16)
    w_ref[0:nc, :] = _dot(xb, wst_ref[...])

    a16 = a16_ref[0:4, :]
    far, fai, bar, bai = a16[0:1], a16[1:2], a16[2:3], a16[3:4]
    zero = jnp.zeros((1, ns), F32)

    def body(c, carry):
        fr_, fi_, br_, bi_ = carry
        cb = nc - 1 - c
        s_ref[pl.ds(c, 1), 0:ns] = fr_
        s_ref[pl.ds(c, 1), ns:2 * ns] = fi_
        s_ref[pl.ds(cb, 1), 2 * ns:3 * ns] = br_
        s_ref[pl.ds(cb, 1), 3 * ns:4 * ns] = bi_
        wfr = w_ref[pl.ds(c, 1), 0:ns]
        wfi = w_ref[pl.ds(c, 1), ns:2 * ns]
        wbr = w_ref[pl.ds(cb, 1), 2 * ns:3 * ns]
        wbi = w_ref[pl.ds(cb, 1), 3 * ns:4 * ns]
        return (far * fr_ - fai * fi_ + wfr, far * fi_ + fai * fr_ + wfi,
                bar * br_ - bai * bi_ + wbr, bar * bi_ + bai * br_ + wbi)

    lax.fori_loop(0, nc, body, (zero, zero, zero, zero))

    y = (_dot(xb, m_ref[...]) + _dot_nt(s_ref[0:nc, :].astype(BF16), wout_ref[...])
         + x * jnp.concatenate([dsk_ref[0, 0]] * S5_T, axis=1))
    for t in range(S5_T):
        o_ref[0, pl.ds(t, nc, stride=S5_T), :] = y[:, t * blk:(t + 1) * blk]


def _s5(cols1, prm, lbr, lbi, ctr, cti, dsk, layer):
    bsz, lq, _ = cols1.shape
    nc = lq // S5_T
    ncp = -(-nc // 8) * 8
    ng = S5_WIDTH // 128
    n = S5_T * 128
    ublk = C1_U // 128
    return pl.pallas_call(
        functools.partial(_s5_kernel, nc=nc),
        grid=(ng, bsz),
        in_specs=[
            pl.BlockSpec((1, lq, 128), lambda g, b: (b, 0, ublk + g)),
            pl.BlockSpec((1, 1, 8, S5_NS), lambda g, b: (layer, g, 0, 0)),
            pl.BlockSpec((1, 1, 128, S5_NS), lambda g, b: (layer, g, 0, 0)),
            pl.BlockSpec((1, 1, 128, S5_NS), lambda g, b: (layer, g, 0, 0)),
            pl.BlockSpec((1, 2, 1, 128, S5_NS), lambda g, b: (layer, 0, g, 0, 0)),
            pl.BlockSpec((1, 2, 1, 128, S5_NS), lambda g, b: (layer, 0, g, 0, 0)),
            pl.BlockSpec((1, 1, 1, 128), lambda g, b: (layer, g, 0, 0)),
        ],
        out_specs=pl.BlockSpec((1, lq, 128), lambda g, b: (b, 0, g)),
        out_shape=jax.ShapeDtypeStruct((bsz, lq, S5_WIDTH), F32),
        scratch_shapes=[
            pltpu.VMEM((n, 4 * S5_NS), BF16),
            pltpu.VMEM((n, n), BF16),
            pltpu.VMEM((n, 4 * S5_NS), BF16),
            pltpu.VMEM(((2 * S5_T - 1) * 128, 128), F32),
            pltpu.VMEM((ncp, 4 * S5_NS), F32),
            pltpu.VMEM((ncp, 4 * S5_NS), F32),
            pltpu.VMEM((8, S5_NS), F32),
        ],
        compiler_params=_params(("arbitrary", "arbitrary")),
        name="s5_mixer",
    )(cols1, prm, lbr, lbi, ctr, cti, dsk)


def _gla_kernel(*refs, rev, cpt, nt):
    if rev:
        (q_ref, k_ref, v_ref, glr_ref, wg_ref, bg_ref, of_ref, gate_ref, ng_ref, o_ref, st_ref) = refs
    else:
        (q_ref, k_ref, v_ref, glr_ref, wg_ref, bg_ref, o_ref, st_ref) = refs
    i = pl.program_id(1)
    tile = nt - 1 - i if rev else i
    ch = GLA_CHUNK

    @pl.when(i == 0)
    def _():
        st_ref[...] = jnp.zeros_like(st_ref)

    row = lax.broadcasted_iota(jnp.int32, (ch, ch), 0)
    col = lax.broadcasted_iota(jnp.int32, (ch, ch), 1)
    tri = jnp.where((col >= row) if rev else (col <= row), 1.0, 0.0).astype(F32)
    arow = lax.broadcasted_iota(jnp.int32, (ch, GLA_HEADS * ch), 0)
    acol = lax.broadcasted_iota(jnp.int32, (ch, GLA_HEADS * ch), 1) & (ch - 1)
    causal = (acol >= arow) if rev else (acol <= arow)
    r_k = lax.broadcasted_iota(jnp.int32, (GLA_HEADS * ch, GLA_KEY), 0) >> 6
    c_k = lax.broadcasted_iota(jnp.int32, (GLA_HEADS * ch, GLA_KEY), 1) >> 6
    bm_k = r_k == c_k
    r_v = lax.broadcasted_iota(jnp.int32, (GLA_HEADS * ch, GLA_WIDTH), 0) >> 6
    c_v = lax.broadcasted_iota(jnp.int32, (GLA_HEADS * ch, GLA_WIDTH), 1) >> 7
    bm_v = r_v == c_v
    r_s = lax.broadcasted_iota(jnp.int32, (GLA_WIDTH, GLA_KEY), 0) >> 7
    c_s = lax.broadcasted_iota(jnp.int32, (GLA_WIDTH, GLA_KEY), 1) >> 6
    bm_s = r_s == c_s
    trow = lax.broadcasted_iota(jnp.int32, (ch, 1), 0)

    def chunk(ci, carry):
        cidx = cpt - 1 - ci if rev else ci
        r0 = pl.multiple_of(cidx * ch, ch)
        q = q_ref[0, pl.ds(r0, ch), :] * (GLA_DK ** -0.5)
        k = k_ref[0, pl.ds(r0, ch), :]
        v = v_ref[0, pl.ds(r0, ch), :]
        glr = glr_ref[0, pl.ds(r0, ch), :]
        glin = _dot(glr.astype(BF16), wg_ref[0]) + bg_ref[0]
        g = jax.nn.log_sigmoid(glin) / GLA_GATE_NORM
        t_glob = tile * (cpt * ch) + cidx * ch + trow
        g = jnp.where(t_glob >= PAD, g, 0.0)
        bcum = jnp.dot(tri, g, precision=HIGHEST, preferred_element_type=F32)
        blast = bcum[0:1] if rev else bcum[ch - 1:ch]
        qe = (q * jnp.exp(bcum)).astype(BF16)
        ke = k * jnp.exp(-bcum)
        kd = (k * jnp.exp(blast - bcum)).astype(BF16)
        ke4 = jnp.where(bm_k, jnp.concatenate([ke] * GLA_HEADS, axis=0), 0.0).astype(BF16)
        att = jnp.where(causal, _dot_nt(qe, ke4), 0.0)
        v4 = jnp.where(bm_v, jnp.concatenate([v] * GLA_HEADS, axis=0), 0.0).astype(BF16)
        st = st_ref[...]
        o = _dot(att.astype(BF16), v4) + _dot_nt(qe, st.astype(BF16))
        ds = _dot_tn(v.astype(BF16), kd)
        st_ref[...] = st * jnp.exp(blast) + jnp.where(bm_s, ds, 0.0)
        if rev:
            o = o + of_ref[0, pl.ds(r0, ch), :]
            parts = []
            for h in range(GLA_HEADS):
                oh = o[:, h * GLA_DV:(h + 1) * GLA_DV]
                parts.append(oh * lax.rsqrt(jnp.mean(oh * oh, axis=-1, keepdims=True) + EPS))
            o = jnp.concatenate(parts, axis=1) * ng_ref[0]
            o = o * jax.nn.silu(gate_ref[0, pl.ds(r0, ch), :])
        o_ref[0, pl.ds(r0, ch), :] = o.astype(o_ref.dtype)
        return carry

    lax.fori_loop(0, cpt, chunk, 0)


def _gla_pass(cols1, cols2, wg, bg, layer, rev, cpt, o_fwd=None, norm_g=None):
    bsz, lq, _ = cols1.shape
    tt = cpt * GLA_CHUNK
    nt = lq // tt
    d = 1 if rev else 0
    tmap = (lambda i: nt - 1 - i) if rev else (lambda i: i)
    in_specs = [
        pl.BlockSpec((1, tt, GLA_KEY), lambda b, i: (b, tmap(i), 0)),
        pl.BlockSpec((1, tt, GLA_KEY), lambda b, i: (b, tmap(i), 1)),
        pl.BlockSpec((1, tt, GLA_WIDTH), lambda b, i: (b, tmap(i), C1_V // GLA_WIDTH)),
        pl.BlockSpec((1, tt, 128), lambda b, i: (b, tmap(i), 2 * GLA_KEY // 128)),
        pl.BlockSpec((1, 128, GLA_KEY), lambda b, i: (layer * 2 + d, 0, 0)),
        pl.BlockSpec((1, 1, GLA_KEY), lambda b, i: (layer * 2 + d, 0, 0)),
    ]
    args = [cols2, cols2, cols1, cols2, wg, bg]
    if rev:
        in_specs += [
            pl.BlockSpec((1, tt, GLA_WIDTH), lambda b, i: (b, tmap(i), 0)),
            pl.BlockSpec((1, tt, GLA_WIDTH), lambda b, i: (b, tmap(i), C1_GB // GLA_WIDTH)),
            pl.BlockSpec((1, 1, GLA_WIDTH), lambda b, i: (layer, 0, 0)),
        ]
        args += [o_fwd, cols1, norm_g]
    return pl.pallas_call(
        functools.partial(_gla_kernel, rev=rev, cpt=cpt, nt=nt),
        grid=(bsz, nt),
        in_specs=in_specs,
        out_specs=pl.BlockSpec((1, tt, GLA_WIDTH), lambda b, i: (b, tmap(i), 0)),
        out_shape=jax.ShapeDtypeStruct((bsz, lq, GLA_WIDTH), BF16 if rev else F32),
        scratch_shapes=[pltpu.VMEM((GLA_WIDTH, GLA_KEY), F32)],
        compiler_params=_params(("parallel", "arbitrary")),
        name="gla_bwd" if rev else "gla_fwd",
    )(*args)


def _lru_kernel(*refs, rev, tt, nt):
    if rev:
        (x_ref, xp_ref, xn_ref, cw_ref, cb_ref, w_ref, ba_ref, bx_ref, lam_ref, hf_ref, gate_ref,
         o_ref, xe_ref, a_ref, b_ref, h_ref) = refs
    else:
        (x_ref, xp_ref, xn_ref, cw_ref, cb_ref, w_ref, ba_ref, bx_ref, lam_ref,
         o_ref, xe_ref, a_ref, b_ref, h_ref) = refs
    i = pl.program_id(1)
    tile = nt - 1 - i if rev else i

    @pl.when(i == 0)
    def _():
        h_ref[...] = jnp.zeros_like(h_ref)

    xe_ref[0:HALO, :] = jnp.where(tile == 0, 0.0, xp_ref[0])
    xe_ref[HALO:HALO + tt, :] = x_ref[0]
    xe_ref[HALO + tt:2 * HALO + tt, :] = jnp.where(tile == nt - 1, 0.0, xn_ref[0])
    left = CONV_WIDTH // 2
    cw = cw_ref[0]
    xc = cb_ref[0]
    for j in range(CONV_WIDTH):
        xc = xc + xe_ref[pl.ds(HALO - left + j, tt), :] * cw[j:j + 1]
    sp = jax.nn.softplus(-lam_ref[0])
    valid = (tile * tt + lax.broadcasted_iota(jnp.int32, (tt, 1), 0)) >= PAD
    for n in range(LRU_BLOCKS):
        sl = slice(n * LRU_BLOCK, (n + 1) * LRU_BLOCK)
        xcn = xc[:, sl]
        pre = _dot(xcn.astype(BF16), w_ref[0, n])
        r = jax.nn.sigmoid(pre[:, :LRU_BLOCK] + ba_ref[0][:, sl])
        ig = jax.nn.sigmoid(pre[:, LRU_BLOCK:] + bx_ref[0][:, sl])
        log_a = -LRU_C * r * sp[:, sl]
        one_m_a2 = -jnp.tanh(log_a) * (jnp.exp(2.0 * log_a) + 1.0)
        a_ref[:, sl] = jnp.exp(log_a)
        b_ref[:, sl] = jnp.where(valid, jnp.sqrt(one_m_a2) * (ig * xcn), 0.0)

    def step(s, h):
        rr = tt - 1 - s if rev else s
        h = a_ref[pl.ds(rr, 1), :] * h + b_ref[pl.ds(rr, 1), :]
        b_ref[pl.ds(rr, 1), :] = h
        return h

    h_ref[...] = lax.fori_loop(0, tt, step, h_ref[...])
    if rev:
        o_ref[0] = ((hf_ref[0] + b_ref[...]) * jax.nn.silu(gate_ref[0])).astype(o_ref.dtype)
    else:
        o_ref[0] = b_ref[...]


def _lru_pass(cols1, cw, cb, w, ba, bx, lam, layer, rev, tt, h_fwd=None):
    bsz, lq, _ = cols1.shape
    nt = lq // tt
    d = 1 if rev else 0
    nh = lq // HALO
    th = tt // HALO
    tmap = (lambda i: nt - 1 - i) if rev else (lambda i: i)
    in_specs = [
        pl.BlockSpec((1, tt, LRU_WIDTH), lambda b, i: (b, tmap(i), 0)),
        pl.BlockSpec((1, HALO, LRU_WIDTH), lambda b, i: (b, jnp.maximum(tmap(i) * th - 1, 0), 0)),
        pl.BlockSpec((1, HALO, LRU_WIDTH), lambda b, i: (b, jnp.minimum((tmap(i) + 1) * th, nh - 1), 0)),
        pl.BlockSpec((1, 8, LRU_WIDTH), lambda b, i: (layer, 0, 0)),
        pl.BlockSpec((1, 1, LRU_WIDTH), lambda b, i: (layer, 0, 0)),
        pl.BlockSpec((1, LRU_BLOCKS, LRU_BLOCK, 2 * LRU_BLOCK), lambda b, i: (layer * 2 + d, 0, 0, 0)),
        pl.BlockSpec((1, 1, LRU_WIDTH), lambda b, i: (layer * 2 + d, 0, 0)),
        pl.BlockSpec((1, 1, LRU_WIDTH), lambda b, i: (layer * 2 + d, 0, 0)),
        pl.BlockSpec((1, 1, LRU_WIDTH), lambda b, i: (layer * 2 + d, 0, 0)),
    ]
    args = [cols1, cols1, cols1, cw, cb, w, ba, bx, lam]
    if rev:
        in_specs += [
            pl.BlockSpec((1, tt, LRU_WIDTH), lambda b, i: (b, tmap(i), 0)),
            pl.BlockSpec((1, tt, LRU_WIDTH), lambda b, i: (b, tmap(i), C1_GC // LRU_WIDTH)),
        ]
        args += [h_fwd, cols1]
    return pl.pallas_call(
        functools.partial(_lru_kernel, rev=rev, tt=tt, nt=nt),
        grid=(bsz, nt),
        in_specs=in_specs,
        out_specs=pl.BlockSpec((1, tt, LRU_WIDTH), lambda b, i: (b, tmap(i), 0)),
        out_shape=jax.ShapeDtypeStruct((bsz, lq, LRU_WIDTH), BF16 if rev else F32),
        scratch_shapes=[
            pltpu.VMEM((tt + 2 * HALO, LRU_WIDTH), F32),
            pltpu.VMEM((tt, LRU_WIDTH), F32),
            pltpu.VMEM((tt, LRU_WIDTH), F32),
            pltpu.VMEM((1, LRU_WIDTH), F32),
        ],
        compiler_params=_params(("parallel", "arbitrary")),
        name="lru_bwd" if rev else "lru_fwd",
    )(*args)


def _out_kernel(z_ref, g_ref, mask_ref, ys_ref, ga_ref, yb_ref, yc_ref, wglu_ref, bglu_ref,
                wm0_ref, wm1_ref, wm2_ref, woa_ref, wob_ref, woc_ref, wo_ref, fg_ref,
                o_ref, h_ref, ya_ref, *, final):
    j = pl.program_id(1)
    nj = pl.num_programs(1)

    @pl.when(j == 0)
    def _():
        z = z_ref[...]
        ms = jnp.mean(z * z, axis=-1, keepdims=True)
        h_ref[...] = (z * lax.rsqrt(ms + EPS) * g_ref[...]).astype(BF16)
        zz = jax.nn.gelu(ys_ref[...])
        glu = jax.nn.sigmoid(_dot(zz.astype(BF16), wglu_ref[0]) + bglu_ref[0])
        ya_ref[...] = (zz * glu * jax.nn.silu(ga_ref[...])).astype(BF16)

    h = h_ref[...]
    m = (jax.nn.sigmoid(_dot(h, wm0_ref[0])) * _dot(ya_ref[...], woa_ref[0])
         + jax.nn.sigmoid(_dot(h, wm1_ref[0])) * _dot(yb_ref[...], wob_ref[0])
         + jax.nn.sigmoid(_dot(h, wm2_ref[0])) * _dot(yc_ref[...], woc_ref[0]))
    contrib = _dot(m.astype(BF16), wo_ref[0])

    @pl.when(j == 0)
    def _():
        o_ref[...] = contrib

    @pl.when(j > 0)
    def _():
        o_ref[...] += contrib

    @pl.when(j == nj - 1)
    def _():
        zn = z_ref[...] + o_ref[...]
        if final:
            ms = jnp.mean(zn * zn, axis=-1, keepdims=True)
            o_ref[...] = zn * lax.rsqrt(ms + EPS) * fg_ref[...]
        else:
            o_ref[...] = jnp.where(mask_ref[...] > 0.0, zn, 0.0)


def _out_proj(z, g, mask, cols1, ys, yb, yc, wglu, bglu, wm, woa, wob, woc, wo, fg, layer, tm, final):
    r, d = z.shape
    nj = d // TJ
    nmj = d // TJ
    return pl.pallas_call(
        functools.partial(_out_kernel, final=final),
        grid=(r // tm, nj),
        in_specs=[
            pl.BlockSpec((tm, d), lambda i, j: (i, 0)),
            pl.BlockSpec((1, d), lambda i, j: (0, 0)),
            pl.BlockSpec((tm, 1), lambda i, j: (i, 0)),
            pl.BlockSpec((tm, S5_WIDTH), lambda i, j: (i, 0)),
            pl.BlockSpec((tm, S5_WIDTH), lambda i, j: (i, C1_GA // S5_WIDTH)),
            pl.BlockSpec((tm, GLA_WIDTH), lambda i, j: (i, 0)),
            pl.BlockSpec((tm, LRU_WIDTH), lambda i, j: (i, 0)),
            pl.BlockSpec((1, S5_WIDTH, S5_WIDTH), lambda i, j: (layer, 0, 0)),
            pl.BlockSpec((1, 1, S5_WIDTH), lambda i, j: (layer, 0, 0)),
            pl.BlockSpec((1, d, TJ), lambda i, j: (layer, 0, j)),
            pl.BlockSpec((1, d, TJ), lambda i, j: (layer, 0, nmj + j)),
            pl.BlockSpec((1, d, TJ), lambda i, j: (layer, 0, 2 * nmj + j)),
            pl.BlockSpec((1, S5_WIDTH, TJ), lambda i, j: (layer, 0, j)),
            pl.BlockSpec((1, GLA_WIDTH, TJ), lambda i, j: (layer, 0, j)),
            pl.BlockSpec((1, LRU_WIDTH, TJ), lambda i, j: (layer, 0, j)),
            pl.BlockSpec((1, TJ, d), lambda i, j: (layer, j, 0)),
            pl.BlockSpec((1, d), lambda i, j: (0, 0)),
        ],
        out_specs=pl.BlockSpec((tm, d), lambda i, j: (i, 0)),
        out_shape=jax.ShapeDtypeStruct((r, d), F32),
        scratch_shapes=[
            pltpu.VMEM((tm, d), BF16),
            pltpu.VMEM((tm, S5_WIDTH), BF16),
        ],
        compiler_params=_params(("parallel", "arbitrary")),
        name="out_proj",
    )(z, g, mask, ys, cols1, yb, yc, wglu, bglu, wm, wm, wm, woa, wob, woc, wo, fg)


def _prepare(p):
    depth = p["w_in"].shape[0]
    w_in = p["w_in"]
    o = 0
    seg = {}
    for name, width in (("u", 512), ("ga", 512), ("q", 256), ("k", 256), ("v", 512), ("gb", 512),
                        ("glr", 2 * GLA_RANK), ("x", 1024), ("gc", 1024), ("m", 3 * w_in.shape[1])):
        seg[name] = w_in[:, :, o:o + width]
        o += width
    w1 = jnp.concatenate([seg["x"], seg["gc"], seg["u"], seg["ga"], seg["v"], seg["gb"]], axis=-1).astype(BF16)
    zpad = jnp.zeros(w_in.shape[:2] + (N2 - 2 * GLA_KEY - 2 * GLA_RANK,), w_in.dtype)
    w2 = jnp.concatenate([seg["q"], seg["k"], seg["glr"], zpad], axis=-1).astype(BF16)
    wm = seg["m"].astype(BF16)

    ng = S5_WIDTH // 128
    eye = jnp.eye(S5_GB, dtype=F32)

    def rows(a):
        return a.astype(F32).reshape(depth, 2, ng, S5_NS)

    ls = jnp.broadcast_to(p["s5_log_step"].astype(F32)[..., None], p["s5_lam_re"].shape)
    lr, li, ls = rows(p["s5_lam_re"]), rows(p["s5_lam_im"]), rows(ls)
    zero = jnp.zeros_like(lr[:, 0])
    prm = jnp.stack([lr[:, 0], li[:, 0], ls[:, 0], lr[:, 1], li[:, 1], ls[:, 1], zero, zero], axis=2)

    def emb_b(bm):
        bm = bm.astype(F32).reshape(depth, ng, S5_GB, 64, 16)
        return jnp.einsum("lGgnc,gh->lGgchn", bm, eye).reshape(depth, ng, 128, S5_NS)

    def emb_c(cm):
        cm = cm.astype(F32).reshape(depth, 2, ng, S5_GB, 16, 64)
        return jnp.einsum("ldGgcn,gh->ldGgchn", cm, eye).reshape(depth, 2, ng, 128, S5_NS)

    wg = jnp.zeros((depth, 2, 128, GLA_KEY), F32)
    for d in range(2):
        wg = wg.at[:, d, d * GLA_RANK:(d + 1) * GLA_RANK, :].set(p["gla_w_gate_up"][:, d].astype(F32))
    lru_w = jnp.concatenate([p["lru_w_a"], p["lru_w_x"]], axis=-1).astype(BF16)
    cw = jnp.concatenate([p["conv_w"].astype(F32),
                          jnp.zeros((depth, 8 - CONV_WIDTH, LRU_WIDTH), F32)], axis=1)
    return dict(
        norm_g=p["norm_g"].astype(F32), w1=w1, w2=w2, wm=wm,
        s5_prm=prm, s5_lbr=emb_b(p["s5_b_re"]), s5_lbi=emb_b(p["s5_b_im"]),
        s5_ctr=emb_c(p["s5_c_re"]), s5_cti=emb_c(p["s5_c_im"]),
        s5_d=p["s5_d"].astype(F32).reshape(depth, ng, 1, 128),
        wglu=p["s5_w_glu"].astype(BF16), bglu=p["s5_b_glu"].astype(F32).reshape(depth, 1, S5_WIDTH),
        wg=wg.reshape(depth * 2, 128, GLA_KEY).astype(BF16),
        bg=p["gla_b_gate"].astype(F32).reshape(depth * 2, 1, GLA_KEY),
        gla_ng=p["gla_norm_g"].astype(F32).reshape(depth, 1, GLA_WIDTH),
        cw=cw, cb=p["conv_b"].astype(F32).reshape(depth, 1, LRU_WIDTH),
        lru_w=lru_w.reshape(depth * 2, LRU_BLOCKS, LRU_BLOCK, 2 * LRU_BLOCK),
        lru_ba=p["lru_b_a"].astype(F32).reshape(depth * 2, 1, LRU_WIDTH),
        lru_bx=p["lru_b_x"].astype(F32).reshape(depth * 2, 1, LRU_WIDTH),
        lru_lam=p["lru_lam"].astype(F32).reshape(depth * 2, 1, LRU_WIDTH),
        woa=p["w_out_a"].astype(BF16), wob=p["w_out_b"].astype(BF16), woc=p["w_out_c"].astype(BF16),
        wo=p["w_o"].astype(BF16), fg=p["final_norm_g"].astype(F32).reshape(1, -1),
    )


def _pick_tile(n, unit, target):
    best = unit
    for t in range(unit, target + 1, unit):
        if n % t == 0:
            best = t
    return best


def _encoder(x, meta, w):
    bsz, seq, d = x.shape
    depth = w["w1"].shape[0]
    lq = PAD + N_META + seq
    r = bsz * lq
    head = jnp.concatenate([jnp.zeros((PAD, d), F32), meta.astype(F32)], axis=0)
    z = jnp.concatenate([jnp.broadcast_to(head[None], (bsz, PAD + N_META, d)), x.astype(F32)], axis=1)
    z = z.reshape(r, d)
    mask = jnp.broadcast_to((jnp.arange(lq) >= PAD).astype(F32)[None, :, None], (bsz, lq, 1)).reshape(r, 1)
    tm = _pick_tile(r, 128, 768)
    tm_out = _pick_tile(r, 128, 640)
    cpt = _pick_tile(lq // GLA_CHUNK, 1, 8)
    tt = _pick_tile(lq, 64, 512)
    for l in range(depth):
        cols1, cols2 = _in_proj(z, w["norm_g"][l][None], w["w1"][l], w["w2"][l], tm)
        c1 = cols1.reshape(bsz, lq, N1)
        c2 = cols2.reshape(bsz, lq, N2)
        ys = _s5(c1, w["s5_prm"], w["s5_lbr"], w["s5_lbi"], w["s5_ctr"], w["s5_cti"], w["s5_d"], l)
        of = _gla_pass(c1, c2, w["wg"], w["bg"], l, False, cpt)
        yb = _gla_pass(c1, c2, w["wg"], w["bg"], l, True, cpt, of, w["gla_ng"])
        hf = _lru_pass(c1, w["cw"], w["cb"], w["lru_w"], w["lru_ba"], w["lru_bx"], w["lru_lam"], l, False, tt)
        yc = _lru_pass(c1, w["cw"], w["cb"], w["lru_w"], w["lru_ba"], w["lru_bx"], w["lru_lam"], l, True, tt, hf)
        z = _out_proj(z, w["norm_g"][l][None], mask, cols1, ys.reshape(r, -1), yb.reshape(r, -1),
                      yc.reshape(r, -1), w["wglu"], w["bglu"], w["wm"], w["woa"], w["wob"], w["woc"],
                      w["wo"], w["fg"], l, tm_out, l == depth - 1)
    return z.reshape(bsz, lq, d)[:, PAD + N_META:].astype(x.dtype)


def kernel(x_prompt, x_sample, meta_tokens, norm_g, w_in, s5_lam_re, s5_lam_im, s5_log_step, s5_b_re, s5_b_im, s5_c_re, s5_c_im, s5_d, s5_w_glu, s5_b_glu, gla_w_gate_up, gla_b_gate, gla_norm_g, conv_w, conv_b, lru_w_a, lru_b_a, lru_w_x, lru_b_x, lru_lam, w_out_a, w_out_b, w_out_c, w_o, final_norm_g):
    w = _prepare(dict(
        norm_g=norm_g, w_in=w_in, s5_lam_re=s5_lam_re, s5_lam_im=s5_lam_im, s5_log_step=s5_log_step,
        s5_b_re=s5_b_re, s5_b_im=s5_b_im, s5_c_re=s5_c_re, s5_c_im=s5_c_im, s5_d=s5_d,
        s5_w_glu=s5_w_glu, s5_b_glu=s5_b_glu, gla_w_gate_up=gla_w_gate_up, gla_b_gate=gla_b_gate,
        gla_norm_g=gla_norm_g, conv_w=conv_w, conv_b=conv_b, lru_w_a=lru_w_a, lru_b_a=lru_b_a,
        lru_w_x=lru_w_x, lru_b_x=lru_b_x, lru_lam=lru_lam, w_out_a=w_out_a, w_out_b=w_out_b,
        w_out_c=w_out_c, w_o=w_o, final_norm_g=final_norm_g))
    return (_encoder(x_prompt, meta_tokens, w), _encoder(x_sample, meta_tokens, w))
```

```python
import functools

import jax
import jax.numpy as jnp
from jax import lax
from jax.experimental import pallas as pl
from jax.experimental.pallas import tpu as pltpu

F32 = jnp.float32
BF16 = jnp.bfloat16
HIGHEST = lax.Precision.HIGHEST

N_META = 16
PAD = 48
EPS = 1e-6
S5_WIDTH = 512
S5_T = 16
S5_GB = 8
S5_NS = 512
GLA_HEADS = 4
GLA_DK = 64
GLA_DV = 128
GLA_KEY = 256
GLA_WIDTH = 512
GLA_RANK = 16
GLA_GATE_NORM = 16.0
GLA_CHUNK = 64
LRU_WIDTH = 1024
LRU_BLOCKS = 8
LRU_BLOCK = 128
CONV_WIDTH = 4
LRU_C = 8.0
HALO = 8
VMEM_LIMIT = 56 * 1024 * 1024

C1_X, C1_GC, C1_U, C1_GA, C1_V, C1_GB = 0, 1024, 2048, 2560, 3072, 3584
N1 = 4096
N2 = 640
TN = 1024
NJ1 = N1 // TN
TJ = 256


def _dot(a, b):
    return jnp.dot(a, b, preferred_element_type=F32)


def _dot_nt(a, b):
    return lax.dot_general(a, b, (((1,), (1,)), ((), ())), preferred_element_type=F32)


def _dot_nt_hi(a, b):
    return lax.dot_general(a, b, (((1,), (1,)), ((), ())), precision=HIGHEST, preferred_element_type=F32)


def _dot_tn(a, b):
    return lax.dot_general(a, b, (((0,), (0,)), ((), ())), preferred_element_type=F32)


def _sigmoid(x):
    return 0.5 * jnp.tanh(0.5 * x) + 0.5


def _params(sem):
    return pltpu.CompilerParams(dimension_semantics=sem, vmem_limit_bytes=VMEM_LIMIT)


def _kin_kernel(z_ref, g_ref, w1_ref, w2_ref, o1_ref, o2_ref, h_ref):
    j = pl.program_id(1)

    @pl.when(j == 0)
    def _():
        z = z_ref[...]
        ms = jnp.mean(z * z, axis=-1, keepdims=True)
        h_ref[...] = (z * lax.rsqrt(ms + EPS) * g_ref[0]).astype(BF16)

    @pl.when(j < NJ1)
    def _():
        o1_ref[...] = _dot(h_ref[...], w1_ref[0])

    @pl.when(j == NJ1)
    def _():
        o2_ref[...] = _dot(h_ref[...], w2_ref[0])


def _in_proj(z, g, w1, w2, layer, tm):
    r, d = z.shape
    last = NJ1 - 1
    return pl.pallas_call(
        _kin_kernel,
        grid=(r // tm, NJ1 + 1),
        in_specs=[
            pl.BlockSpec((tm, d), lambda i, j: (i, 0)),
            pl.BlockSpec((1, 1, d), lambda i, j: (layer, 0, 0)),
            pl.BlockSpec((1, d, TN), lambda i, j: (layer * NJ1 + jnp.minimum(j, last), 0, 0)),
            pl.BlockSpec((1, d, N2), lambda i, j: (layer, 0, 0)),
        ],
        out_specs=[
            pl.BlockSpec((tm, TN), lambda i, j: (i, jnp.minimum(j, last))),
            pl.BlockSpec((tm, N2), lambda i, j: (i, 0)),
        ],
        out_shape=[jax.ShapeDtypeStruct((r, N1), F32), jax.ShapeDtypeStruct((r, N2), F32)],
        scratch_shapes=[pltpu.VMEM((tm, d), BF16)],
        compiler_params=_params(("parallel", "arbitrary")),
        name="in_proj",
    )(z, g, w1, w2)


def _s5_kernel(u_ref, prm_ref, lbr_ref, lbi_ref, ctr_ref, cti_ref, dsk_ref, o_ref,
               wst_ref, m_ref, wout_ref, e_ref, w_ref, s_ref, a16_ref, *, nc):
    b = pl.program_id(1)
    ns = S5_NS
    blk = S5_GB * 16

    @pl.when(b == 0)
    def _build():
        prm = prm_ref[0, 0]
        evec = lax.broadcasted_iota(jnp.int32, (24, 1), 0).astype(F32)
        e_ref[...] = jnp.zeros_like(e_ref)
        for d in range(2):
            lr = prm[3 * d:3 * d + 1]
            li = prm[3 * d + 1:3 * d + 2]
            dt = jnp.exp(prm[3 * d + 2:3 * d + 3])
            mag = jnp.exp(evec * (lr * dt))
            ang = evec * (li * dt)
            pre = mag * jnp.cos(ang)
            pim = mag * jnp.sin(ang)
            abr = pre[1:2]
            abi = pim[1:2]
            den = lr * lr + li * li
            fr = ((abr - 1.0) * lr + abi * li) / den
            fi = (abi * lr - (abr - 1.0) * li) / den
            lbr = lbr_ref[0, 0]
            lbi = lbi_ref[0, 0]
            bbr = fr * lbr - fi * lbi
            bbi = fr * lbi + fi * lbr
            ctr = ctr_ref[0, d, 0]
            cti = cti_ref[0, d, 0]
            a16_ref[2 * d:2 * d + 1, :] = pre[S5_T:S5_T + 1]
            a16_ref[2 * d + 1:2 * d + 2, :] = pim[S5_T:S5_T + 1]
            for r in range(S5_T):
                e = S5_T - 1 - r if d == 0 else r
                pr_e = pre[e:e + 1]
                pi_e = pim[e:e + 1]
                l_re = bbr * pr_e - bbi * pi_e
                l_im = bbr * pi_e + bbi * pr_e
                base = 2 * d * ns
                wst_ref[r * blk:(r + 1) * blk, base:base + ns] = l_re.astype(BF16)
                wst_ref[r * blk:(r + 1) * blk, base + ns:base + 2 * ns] = l_im.astype(BF16)
                kblk = _dot_nt_hi(l_re, ctr) - _dot_nt_hi(l_im, cti)
                eb = r if d == 0 else S5_T - 1 + r
                e_ref[eb * blk:(eb + 1) * blk, :] += kblk
                eo = r + 1 if d == 0 else S5_T - r
                pr_o = pre[eo:eo + 1]
                pi_o = pim[eo:eo + 1]
                wout_ref[r * blk:(r + 1) * blk, base:base + ns] = (pr_o * ctr - pi_o * cti).astype(BF16)
                wout_ref[r * blk:(r + 1) * blk, base + ns:base + 2 * ns] = (
                    -(pi_o * ctr) - pr_o * cti).astype(BF16)
        for t in range(S5_T):
            off = (S5_T - 1 - t) * blk
            m_ref[:, t * blk:(t + 1) * blk] = e_ref[off:off + S5_T * blk, :].astype(BF16)

    x = jnp.concatenate([u_ref[0, pl.ds(t, nc, stride=S5_T), :] for t in range(S5_T)], axis=1)
    xb = x.astype(BF16)
    w_ref[0:nc, :] = _dot(xb, wst_ref[...])

    a16 = a16_ref[0:4, :]
    far, fai, bar, bai = a16[0:1], a16[1:2], a16[2:3], a16[3:4]
    zero = jnp.zeros((1, ns), F32)

    def body(c, carry):
        fr_, fi_, br_, bi_ = carry
        cb = nc - 1 - c
        s_ref[pl.ds(c, 1), 0:ns] = fr_
        s_ref[pl.ds(c, 1), ns:2 * ns] = fi_
        s_ref[pl.ds(cb, 1), 2 * ns:3 * ns] = br_
        s_ref[pl.ds(cb, 1), 3 * ns:4 * ns] = bi_
        wfr = w_ref[pl.ds(c, 1), 0:ns]
        wfi = w_ref[pl.ds(c, 1), ns:2 * ns]
        wbr = w_ref[pl.ds(cb, 1), 2 * ns:3 * ns]
        wbi = w_ref[pl.ds(cb, 1), 3 * ns:4 * ns]
        return (far * fr_ - fai * fi_ + wfr, far * fi_ + fai * fr_ + wfi,
                bar * br_ - bai * bi_ + wbr, bar * bi_ + bai * br_ + wbi)

    lax.fori_loop(0, nc, body, (zero, zero, zero, zero))

    y = (_dot(xb, m_ref[...]) + _dot_nt(s_ref[0:nc, :].astype(BF16), wout_ref[...])
         + x * jnp.concatenate([dsk_ref[0, 0]] * S5_T, axis=1))
    for t in range(S5_T):
        o_ref[0, pl.ds(t, nc, stride=S5_T), :] = y[:, t * blk:(t + 1) * blk]


def _s5(cols1, prm, lbr, lbi, ctr, cti, dsk, layer):
    bsz, lq, _ = cols1.shape
    nc = lq // S5_T
    ncp = -(-nc // 8) * 8
    ng = S5_WIDTH // 128
    n = S5_T * 128
    ublk = C1_U // 128
    return pl.pallas_call(
        functools.partial(_s5_kernel, nc=nc),
        grid=(ng, bsz),
        in_specs=[
            pl.BlockSpec((1, lq, 128), lambda g, b: (b, 0, ublk + g)),
            pl.BlockSpec((1, 1, 8, S5_NS), lambda g, b: (layer, g, 0, 0)),
            pl.BlockSpec((1, 1, 128, S5_NS), lambda g, b: (layer, g, 0, 0)),
            pl.BlockSpec((1, 1, 128, S5_NS), lambda g, b: (layer, g, 0, 0)),
            pl.BlockSpec((1, 2, 1, 128, S5_NS), lambda g, b: (layer, 0, g, 0, 0)),
            pl.BlockSpec((1, 2, 1, 128, S5_NS), lambda g, b: (layer, 0, g, 0, 0)),
            pl.BlockSpec((1, 1, 1, 128), lambda g, b: (layer, g, 0, 0)),
        ],
        out_specs=pl.BlockSpec((1, lq, 128), lambda g, b: (b, 0, g)),
        out_shape=jax.ShapeDtypeStruct((bsz, lq, S5_WIDTH), F32),
        scratch_shapes=[
            pltpu.VMEM((n, 4 * S5_NS), BF16),
            pltpu.VMEM((n, n), BF16),
            pltpu.VMEM((n, 4 * S5_NS), BF16),
            pltpu.VMEM(((2 * S5_T - 1) * 128, 128), F32),
            pltpu.VMEM((ncp, 4 * S5_NS), F32),
            pltpu.VMEM((ncp, 4 * S5_NS), F32),
            pltpu.VMEM((8, S5_NS), F32),
        ],
        compiler_params=_params(("arbitrary", "arbitrary")),
        name="s5_mixer",
    )(cols1, prm, lbr, lbi, ctr, cti, dsk)


def _gla_kernel(*refs, rev, cpt, nt):
    if rev:
        (q_ref, k_ref, v_ref, glr_ref, wg_ref, bg_ref, of_ref, gate_ref, ng_ref, o_ref, st_ref) = refs
    else:
        (q_ref, k_ref, v_ref, glr_ref, wg_ref, bg_ref, o_ref, st_ref) = refs
    i = pl.program_id(1)
    tile = nt - 1 - i if rev else i
    ch = GLA_CHUNK

    @pl.when(i == 0)
    def _():
        st_ref[...] = jnp.zeros_like(st_ref)

    row = lax.broadcasted_iota(jnp.int32, (ch, ch), 0)
    col = lax.broadcasted_iota(jnp.int32, (ch, ch), 1)
    tri = jnp.where((col >= row) if rev else (col <= row), 1.0, 0.0).astype(F32)
    arow = lax.broadcasted_iota(jnp.int32, (ch, GLA_HEADS * ch), 0)
    acol = lax.broadcasted_iota(jnp.int32, (ch, GLA_HEADS * ch), 1) & (ch - 1)
    causal = (acol >= arow) if rev else (acol <= arow)
    r_k = lax.broadcasted_iota(jnp.int32, (GLA_HEADS * ch, GLA_KEY), 0) >> 6
    c_k = lax.broadcasted_iota(jnp.int32, (GLA_HEADS * ch, GLA_KEY), 1) >> 6
    bm_k = r_k == c_k
    r_v = lax.broadcasted_iota(jnp.int32, (GLA_HEADS * ch, GLA_WIDTH), 0) >> 6
    c_v = lax.broadcasted_iota(jnp.int32, (GLA_HEADS * ch, GLA_WIDTH), 1) >> 7
    bm_v = r_v == c_v
    r_s = lax.broadcasted_iota(jnp.int32, (GLA_WIDTH, GLA_KEY), 0) >> 7
    c_s = lax.broadcasted_iota(jnp.int32, (GLA_WIDTH, GLA_KEY), 1) >> 6
    bm_s = r_s == c_s
    trow = lax.broadcasted_iota(jnp.int32, (ch, 1), 0)

    def chunk(ci):
        cidx = cpt - 1 - ci if rev else ci
        r0 = cidx * ch
        q = q_ref[0, pl.ds(r0, ch), :] * (GLA_DK ** -0.5)
        k = k_ref[0, pl.ds(r0, ch), :]
        v = v_ref[0, pl.ds(r0, ch), :]
        glr = glr_ref[0, pl.ds(r0, ch), :]
        glin = _dot(glr.astype(BF16), wg_ref[0]) + bg_ref[0]
        g = jax.nn.log_sigmoid(glin) / GLA_GATE_NORM
        t_glob = tile * (cpt * ch) + cidx * ch + trow
        g = jnp.where(t_glob >= PAD, g, 0.0)
        bcum = jnp.dot(tri, g, precision=HIGHEST, preferred_element_type=F32)
        blast = bcum[0:1] if rev else bcum[ch - 1:ch]
        qe = (q * jnp.exp(bcum)).astype(BF16)
        ke = k * jnp.exp(-bcum)
        kd = (k * jnp.exp(blast - bcum)).astype(BF16)
        ke4 = jnp.where(bm_k, jnp.concatenate([ke] * GLA_HEADS, axis=0), 0.0).astype(BF16)
        att = jnp.where(causal, _dot_nt(qe, ke4), 0.0)
        v4 = jnp.where(bm_v, jnp.concatenate([v] * GLA_HEADS, axis=0), 0.0).astype(BF16)
        st = st_ref[...]
        o = _dot(att.astype(BF16), v4) + _dot_nt(qe, st.astype(BF16))
        ds = _dot_tn(v.astype(BF16), kd)
        st_ref[...] = st * jnp.exp(blast) + jnp.where(bm_s, ds, 0.0)
        if rev:
            o = o + of_ref[0, pl.ds(r0, ch), :]
            parts = []
            for h in range(GLA_HEADS):
                oh = o[:, h * GLA_DV:(h + 1) * GLA_DV]
                parts.append(oh * lax.rsqrt(jnp.mean(oh * oh, axis=-1, keepdims=True) + EPS))
            o = jnp.concatenate(parts, axis=1) * ng_ref[0]
            gate = gate_ref[0, pl.ds(r0, ch), :]
            o = o * (gate * _sigmoid(gate))
        o_ref[0, pl.ds(r0, ch), :] = o.astype(o_ref.dtype)

    for ci in range(cpt):
        chunk(ci)


def _gla_pass(cols1, cols2, wg, bg, layer, rev, cpt, o_fwd=None, norm_g=None):
    bsz, lq, _ = cols1.shape
    tt = cpt * GLA_CHUNK
    nt = lq // tt
    d = 1 if rev else 0
    tmap = (lambda i: nt - 1 - i) if rev else (lambda i: i)
    in_specs = [
        pl.BlockSpec((1, tt, GLA_KEY), lambda b, i: (b, tmap(i), 0)),
        pl.BlockSpec((1, tt, GLA_KEY), lambda b, i: (b, tmap(i), 1)),
        pl.BlockSpec((1, tt, GLA_WIDTH), lambda b, i: (b, tmap(i), C1_V // GLA_WIDTH)),
        pl.BlockSpec((1, tt, 128), lambda b, i: (b, tmap(i), 2 * GLA_KEY // 128)),
        pl.BlockSpec((1, 128, GLA_KEY), lambda b, i: (layer * 2 + d, 0, 0)),
        pl.BlockSpec((1, 1, GLA_KEY), lambda b, i: (layer * 2 + d, 0, 0)),
    ]
    args = [cols2, cols2, cols1, cols2, wg, bg]
    if rev:
        in_specs += [
            pl.BlockSpec((1, tt, GLA_WIDTH), lambda b, i: (b, tmap(i), 0)),
            pl.BlockSpec((1, tt, GLA_WIDTH), lambda b, i: (b, tmap(i), C1_GB // GLA_WIDTH)),
            pl.BlockSpec((1, 1, GLA_WIDTH), lambda b, i: (layer, 0, 0)),
        ]
        args += [o_fwd, cols1, norm_g]
    return pl.pallas_call(
        functools.partial(_gla_kernel, rev=rev, cpt=cpt, nt=nt),
        grid=(bsz, nt),
        in_specs=in_specs,
        out_specs=pl.BlockSpec((1, tt, GLA_WIDTH), lambda b, i: (b, tmap(i), 0)),
        out_shape=jax.ShapeDtypeStruct((bsz, lq, GLA_WIDTH), BF16 if rev else F32),
        scratch_shapes=[pltpu.VMEM((GLA_WIDTH, GLA_KEY), F32)],
        compiler_params=_params(("parallel", "arbitrary")),
        name="gla_bwd" if rev else "gla_fwd",
    )(*args)


def _lru_kernel(*refs, rev, tt, nt):
    if rev:
        (x_ref, xp_ref, xn_ref, cw_ref, cb_ref, w_ref, ba_ref, bx_ref, lam_ref, hf_ref, gate_ref,
         o_ref, xe_ref, a_ref, b_ref, h_ref) = refs
    else:
        (x_ref, xp_ref, xn_ref, cw_ref, cb_ref, w_ref, ba_ref, bx_ref, lam_ref,
         o_ref, xe_ref, a_ref, b_ref, h_ref) = refs
    i = pl.program_id(1)
    tile = nt - 1 - i if rev else i

    @pl.when(i == 0)
    def _():
        h_ref[...] = jnp.zeros_like(h_ref)

    xe_ref[0:HALO, :] = jnp.where(tile == 0, 0.0, xp_ref[0])
    xe_ref[HALO:HALO + tt, :] = x_ref[0]
    xe_ref[HALO + tt:2 * HALO + tt, :] = jnp.where(tile == nt - 1, 0.0, xn_ref[0])
    left = CONV_WIDTH // 2
    cw = cw_ref[0]
    xc = cb_ref[0]
    for j in range(CONV_WIDTH):
        xc = xc + xe_ref[pl.ds(HALO - left + j, tt), :] * cw[j:j + 1]
    sp = jax.nn.softplus(-lam_ref[0])
    valid = (tile * tt + lax.broadcasted_iota(jnp.int32, (tt, 1), 0)) >= PAD
    for n in range(LRU_BLOCKS):
        sl = slice(n * LRU_BLOCK, (n + 1) * LRU_BLOCK)
        xcn = xc[:, sl]
        pre = _dot(xcn.astype(BF16), w_ref[0, n])
        r = _sigmoid(pre[:, :LRU_BLOCK] + ba_ref[0][:, sl])
        ig = _sigmoid(pre[:, LRU_BLOCK:] + bx_ref[0][:, sl])
        log_a = -LRU_C * r * sp[:, sl]
        a = jnp.exp(log_a)
        one_m_a2 = -jnp.tanh(log_a) * (a * a + 1.0)
        a_ref[n] = a
        b_ref[n] = jnp.where(valid, jnp.sqrt(one_m_a2) * (ig * xcn), 0.0)

    seg = tt // 8

    def local(j, carry):
        hs, ps = carry
        jj = seg - 1 - j if rev else j
        nh, npr = [], []
        for n in range(LRU_BLOCKS):
            av = a_ref[n, pl.ds(jj, 8, stride=seg), :]
            bv = b_ref[n, pl.ds(jj, 8, stride=seg), :]
            h = av * hs[n] + bv
            p = av * ps[n]
            b_ref[n, pl.ds(jj, 8, stride=seg), :] = h
            a_ref[n, pl.ds(jj, 8, stride=seg), :] = p
            nh.append(h)
            npr.append(p)
        return tuple(nh), tuple(npr)

    zeros = tuple(jnp.zeros((8, LRU_BLOCK), F32) for _ in range(LRU_BLOCKS))
    ones = tuple(jnp.ones((8, LRU_BLOCK), F32) for _ in range(LRU_BLOCKS))
    h_end, p_end = lax.fori_loop(0, seg, local, (zeros, ones))

    cins = []
    for n in range(LRU_BLOCKS):
        c = h_ref[n]
        rows = [None] * 8
        for s in (range(7, -1, -1) if rev else range(8)):
            rows[s] = c
            c = p_end[n][s:s + 1] * c + h_end[n][s:s + 1]
        h_ref[n] = c
        cins.append(jnp.concatenate(rows, axis=0))

    def fixup(j, carry):
        for n in range(LRU_BLOCKS):
            idx = (n, pl.ds(j, 8, stride=seg), slice(None))
            b_ref[idx] = b_ref[idx] + a_ref[idx] * cins[n]
        return carry

    lax.fori_loop(0, seg, fixup, 0)
    for n in range(LRU_BLOCKS):
        sl = slice(n * LRU_BLOCK, (n + 1) * LRU_BLOCK)
        if rev:
            g = gate_ref[0, :, sl]
            o_ref[0, :, sl] = ((hf_ref[0, :, sl] + b_ref[n]) * (g * _sigmoid(g))).astype(o_ref.dtype)
        else:
            o_ref[0, :, sl] = b_ref[n]


def _lru_pass(cols1, cw, cb, w, ba, bx, lam, layer, rev, tt, h_fwd=None):
    bsz, lq, _ = cols1.shape
    nt = lq // tt
    d = 1 if rev else 0
    nh = lq // HALO
    th = tt // HALO
    tmap = (lambda i: nt - 1 - i) if rev else (lambda i: i)
    in_specs = [
        pl.BlockSpec((1, tt, LRU_WIDTH), lambda b, i: (b, tmap(i), 0)),
        pl.BlockSpec((1, HALO, LRU_WIDTH), lambda b, i: (b, jnp.maximum(tmap(i) * th - 1, 0), 0)),
        pl.BlockSpec((1, HALO, LRU_WIDTH), lambda b, i: (b, jnp.minimum((tmap(i) + 1) * th, nh - 1), 0)),
        pl.BlockSpec((1, 8, LRU_WIDTH), lambda b, i: (layer, 0, 0)),
        pl.BlockSpec((1, 1, LRU_WIDTH), lambda b, i: (layer, 0, 0)),
        pl.BlockSpec((1, LRU_BLOCKS, LRU_BLOCK, 2 * LRU_BLOCK), lambda b, i: (layer * 2 + d, 0, 0, 0)),
        pl.BlockSpec((1, 1, LRU_WIDTH), lambda b, i: (layer * 2 + d, 0, 0)),
        pl.BlockSpec((1, 1, LRU_WIDTH), lambda b, i: (layer * 2 + d, 0, 0)),
        pl.BlockSpec((1, 1, LRU_WIDTH), lambda b, i: (layer * 2 + d, 0, 0)),
    ]
    args = [cols1, cols1, cols1, cw, cb, w, ba, bx, lam]
    if rev:
        in_specs += [
            pl.BlockSpec((1, tt, LRU_WIDTH), lambda b, i: (b, tmap(i), 0)),
            pl.BlockSpec((1, tt, LRU_WIDTH), lambda b, i: (b, tmap(i), C1_GC // LRU_WIDTH)),
        ]
        args += [h_fwd, cols1]
    return pl.pallas_call(
        functools.partial(_lru_kernel, rev=rev, tt=tt, nt=nt),
        grid=(bsz, nt),
        in_specs=in_specs,
        out_specs=pl.BlockSpec((1, tt, LRU_WIDTH), lambda b, i: (b, tmap(i), 0)),
        out_shape=jax.ShapeDtypeStruct((bsz, lq, LRU_WIDTH), BF16 if rev else F32),
        scratch_shapes=[
            pltpu.VMEM((tt + 2 * HALO, LRU_WIDTH), F32),
            pltpu.VMEM((LRU_BLOCKS, tt, LRU_BLOCK), F32),
            pltpu.VMEM((LRU_BLOCKS, tt, LRU_BLOCK), F32),
            pltpu.VMEM((LRU_BLOCKS, 1, LRU_BLOCK), F32),
        ],
        compiler_params=_params(("parallel", "arbitrary")),
        name="lru_bwd" if rev else "lru_fwd",
    )(*args)


def _out_kernel(z_ref, g_ref, mask_ref, ys_ref, ga_ref, yb_ref, yc_ref, wglu_ref, bglu_ref,
                wm0_ref, wm1_ref, wm2_ref, woa_ref, wob_ref, woc_ref, wo_ref, fg_ref,
                o_ref, h_ref, ya_ref, *, final):
    j = pl.program_id(1)
    nj = pl.num_programs(1)

    @pl.when(j == 0)
    def _():
        z = z_ref[...]
        ms = jnp.mean(z * z, axis=-1, keepdims=True)
        h_ref[...] = (z * lax.rsqrt(ms + EPS) * g_ref[0]).astype(BF16)
        zz = jax.nn.gelu(ys_ref[...])
        glu = _sigmoid(_dot(zz.astype(BF16), wglu_ref[0]) + bglu_ref[0])
        ga = ga_ref[...]
        ya_ref[...] = (zz * glu * (ga * _sigmoid(ga))).astype(BF16)

    h = h_ref[...]
    m = (_sigmoid(_dot(h, wm0_ref[0])) * _dot(ya_ref[...], woa_ref[0])
         + _sigmoid(_dot(h, wm1_ref[0])) * _dot(yb_ref[...], wob_ref[0])
         + _sigmoid(_dot(h, wm2_ref[0])) * _dot(yc_ref[...], woc_ref[0]))
    contrib = _dot(m.astype(BF16), wo_ref[0])

    @pl.when(j == 0)
    def _():
        o_ref[...] = contrib

    @pl.when(j > 0)
    def _():
        o_ref[...] += contrib

    @pl.when(j == nj - 1)
    def _():
        zn = z_ref[...] + o_ref[...]
        if final:
            ms = jnp.mean(zn * zn, axis=-1, keepdims=True)
            o_ref[...] = zn * lax.rsqrt(ms + EPS) * fg_ref[...]
        else:
            o_ref[...] = jnp.where(mask_ref[...] > 0.0, zn, 0.0)


def _out_proj(z, g, mask, cols1, ys, yb, yc, wglu, bglu, wm, woa, wob, woc, wo, fg, layer, tm, final):
    r, d = z.shape
    nj = d // TJ
    return pl.pallas_call(
        functools.partial(_out_kernel, final=final),
        grid=(r // tm, nj),
        in_specs=[
            pl.BlockSpec((tm, d), lambda i, j: (i, 0)),
            pl.BlockSpec((1, 1, d), lambda i, j: (layer, 0, 0)),
            pl.BlockSpec((tm, 1), lambda i, j: (i, 0)),
            pl.BlockSpec((tm, S5_WIDTH), lambda i, j: (i, 0)),
            pl.BlockSpec((tm, S5_WIDTH), lambda i, j: (i, C1_GA // S5_WIDTH)),
            pl.BlockSpec((tm, GLA_WIDTH), lambda i, j: (i, 0)),
            pl.BlockSpec((tm, LRU_WIDTH), lambda i, j: (i, 0)),
            pl.BlockSpec((1, S5_WIDTH, S5_WIDTH), lambda i, j: (layer, 0, 0)),
            pl.BlockSpec((1, 1, S5_WIDTH), lambda i, j: (layer, 0, 0)),
            pl.BlockSpec((1, d, TJ), lambda i, j: (layer * 3 * nj + j, 0, 0)),
            pl.BlockSpec((1, d, TJ), lambda i, j: (layer * 3 * nj + nj + j, 0, 0)),
            pl.BlockSpec((1, d, TJ), lambda i, j: (layer * 3 * nj + 2 * nj + j, 0, 0)),
            pl.BlockSpec((1, S5_WIDTH, TJ), lambda i, j: (layer * nj + j, 0, 0)),
            pl.BlockSpec((1, GLA_WIDTH, TJ), lambda i, j: (layer * nj + j, 0, 0)),
            pl.BlockSpec((1, LRU_WIDTH, TJ), lambda i, j: (layer * nj + j, 0, 0)),
            pl.BlockSpec((1, TJ, d), lambda i, j: (layer, j, 0)),
            pl.BlockSpec((1, d), lambda i, j: (0, 0)),
        ],
        out_specs=pl.BlockSpec((tm, d), lambda i, j: (i, 0)),
        out_shape=jax.ShapeDtypeStruct((r, d), F32),
        scratch_shapes=[
            pltpu.VMEM((tm, d), BF16),
            pltpu.VMEM((tm, S5_WIDTH), BF16),
        ],
        compiler_params=_params(("parallel", "arbitrary")),
        name="out_proj",
    )(z, g, mask, ys, cols1, yb, yc, wglu, bglu, wm, wm, wm, woa, wob, woc, wo, fg)


def _prepare(p):
    depth = p["w_in"].shape[0]
    w_in = p["w_in"]
    o = 0
    seg = {}
    for name, width in (("u", 512), ("ga", 512), ("q", 256), ("k", 256), ("v", 512), ("gb", 512),
                        ("glr", 2 * GLA_RANK), ("x", 1024), ("gc", 1024), ("m", 3 * w_in.shape[1])):
        seg[name] = w_in[:, :, o:o + width]
        o += width
    w1 = jnp.concatenate([seg["x"], seg["gc"], seg["u"], seg["ga"], seg["v"], seg["gb"]], axis=-1).astype(BF16)
    zpad = jnp.zeros(w_in.shape[:2] + (N2 - 2 * GLA_KEY - 2 * GLA_RANK,), w_in.dtype)
    w2 = jnp.concatenate([seg["q"], seg["k"], seg["glr"], zpad], axis=-1).astype(BF16)
    wm = seg["m"].astype(BF16)

    def col_tiles(a, width):
        dep, k, n = a.shape
        return a.reshape(dep, k, n // width, width).transpose(0, 2, 1, 3).reshape(dep * (n // width), k, width)

    w1 = col_tiles(w1, TN)
    wm = col_tiles(wm, TJ)

    ng = S5_WIDTH // 128
    eye = jnp.eye(S5_GB, dtype=F32)

    def rows(a):
        return a.astype(F32).reshape(depth, 2, ng, S5_NS)

    ls = jnp.broadcast_to(p["s5_log_step"].astype(F32)[..., None], p["s5_lam_re"].shape)
    lr, li, ls = rows(p["s5_lam_re"]), rows(p["s5_lam_im"]), rows(ls)
    zero = jnp.zeros_like(lr[:, 0])
    prm = jnp.stack([lr[:, 0], li[:, 0], ls[:, 0], lr[:, 1], li[:, 1], ls[:, 1], zero, zero], axis=2)

    def emb_b(bm):
        bm = bm.astype(F32).reshape(depth, ng, S5_GB, 64, 16)
        return jnp.einsum("lGgnc,gh->lGgchn", bm, eye).reshape(depth, ng, 128, S5_NS)

    def emb_c(cm):
        cm = cm.astype(F32).reshape(depth, 2, ng, S5_GB, 16, 64)
        return jnp.einsum("ldGgcn,gh->ldGgchn", cm, eye).reshape(depth, 2, ng, 128, S5_NS)

    wg = jnp.zeros((depth, 2, 128, GLA_KEY), F32)
    for d in range(2):
        wg = wg.at[:, d, d * GLA_RANK:(d + 1) * GLA_RANK, :].set(p["gla_w_gate_up"][:, d].astype(F32))
    lru_w = jnp.concatenate([p["lru_w_a"], p["lru_w_x"]], axis=-1).astype(BF16)
    cw = jnp.concatenate([p["conv_w"].astype(F32),
                          jnp.zeros((depth, 8 - CONV_WIDTH, LRU_WIDTH), F32)], axis=1)
    return dict(
        norm_g=p["norm_g"].astype(F32).reshape(depth, 1, -1), w1=w1, w2=w2, wm=wm,
        s5_prm=prm, s5_lbr=emb_b(p["s5_b_re"]), s5_lbi=emb_b(p["s5_b_im"]),
        s5_ctr=emb_c(p["s5_c_re"]), s5_cti=emb_c(p["s5_c_im"]),
        s5_d=p["s5_d"].astype(F32).reshape(depth, ng, 1, 128),
        wglu=p["s5_w_glu"].astype(BF16), bglu=p["s5_b_glu"].astype(F32).reshape(depth, 1, S5_WIDTH),
        wg=wg.reshape(depth * 2, 128, GLA_KEY).astype(BF16),
        bg=p["gla_b_gate"].astype(F32).reshape(depth * 2, 1, GLA_KEY),
        gla_ng=p["gla_norm_g"].astype(F32).reshape(depth, 1, GLA_WIDTH),
        cw=cw, cb=p["conv_b"].astype(F32).reshape(depth, 1, LRU_WIDTH),
        lru_w=lru_w.reshape(depth * 2, LRU_BLOCKS, LRU_BLOCK, 2 * LRU_BLOCK),
        lru_ba=p["lru_b_a"].astype(F32).reshape(depth * 2, 1, LRU_WIDTH),
        lru_bx=p["lru_b_x"].astype(F32).reshape(depth * 2, 1, LRU_WIDTH),
        lru_lam=p["lru_lam"].astype(F32).reshape(depth * 2, 1, LRU_WIDTH),
        woa=col_tiles(p["w_out_a"].astype(BF16), TJ), wob=col_tiles(p["w_out_b"].astype(BF16), TJ),
        woc=col_tiles(p["w_out_c"].astype(BF16), TJ),
        wo=p["w_o"].astype(BF16), fg=p["final_norm_g"].astype(F32).reshape(1, -1),
    )


def _pick_tile(n, unit, target):
    best = unit
    for t in range(unit, target + 1, unit):
        if n % t == 0:
            best = t
    return best


def _encoder(x, meta, w):
    bsz, seq, d = x.shape
    depth = w["w2"].shape[0]
    lq = PAD + N_META + seq
    r = bsz * lq
    head = jnp.concatenate([jnp.zeros((PAD, d), F32), meta.astype(F32)], axis=0)
    z = jnp.concatenate([jnp.broadcast_to(head[None], (bsz, PAD + N_META, d)), x.astype(F32)], axis=1)
    z = z.reshape(r, d)
    mask = jnp.broadcast_to((jnp.arange(lq) >= PAD).astype(F32)[None, :, None], (bsz, lq, 1)).reshape(r, 1)
    tm = _pick_tile(r, 128, 768)
    tm_out = _pick_tile(r, 128, 640)
    cpt = _pick_tile(lq // GLA_CHUNK, 1, 8)
    tt = _pick_tile(lq, 64, 512)
    for l in range(depth):
        cols1, cols2 = _in_proj(z, w["norm_g"], w["w1"], w["w2"], l, tm)
        c1 = cols1.reshape(bsz, lq, N1)
        c2 = cols2.reshape(bsz, lq, N2)
        ys = _s5(c1, w["s5_prm"], w["s5_lbr"], w["s5_lbi"], w["s5_ctr"], w["s5_cti"], w["s5_d"], l)
        of = _gla_pass(c1, c2, w["wg"], w["bg"], l, False, cpt)
        yb = _gla_pass(c1, c2, w["wg"], w["bg"], l, True, cpt, of, w["gla_ng"])
        hf = _lru_pass(c1, w["cw"], w["cb"], w["lru_w"], w["lru_ba"], w["lru_bx"], w["lru_lam"], l, False, tt)
        yc = _lru_pass(c1, w["cw"], w["cb"], w["lru_w"], w["lru_ba"], w["lru_bx"], w["lru_lam"], l, True, tt, hf)
        z = _out_proj(z, w["norm_g"], mask, cols1, ys.reshape(r, -1), yb.reshape(r, -1),
                      yc.reshape(r, -1), w["wglu"], w["bglu"], w["wm"], w["woa"], w["wob"], w["woc"],
                      w["wo"], w["fg"], l, tm_out, l == depth - 1)
    return z.reshape(bsz, lq, d)[:, PAD + N_META:].astype(x.dtype)


def kernel(x_prompt, x_sample, meta_tokens, norm_g, w_in, s5_lam_re, s5_lam_im, s5_log_step, s5_b_re, s5_b_im, s5_c_re, s5_c_im, s5_d, s5_w_glu, s5_b_glu, gla_w_gate_up, gla_b_gate, gla_norm_g, conv_w, conv_b, lru_w_a, lru_b_a, lru_w_x, lru_b_x, lru_lam, w_out_a, w_out_b, w_out_c, w_o, final_norm_g):
    w = _prepare(dict(
        norm_g=norm_g, w_in=w_in, s5_lam_re=s5_lam_re, s5_lam_im=s5_lam_im, s5_log_step=s5_log_step,
        s5_b_re=s5_b_re, s5_b_im=s5_b_im, s5_c_re=s5_c_re, s5_c_im=s5_c_im, s5_d=s5_d,
        s5_w_glu=s5_w_glu, s5_b_glu=s5_b_glu, gla_w_gate_up=gla_w_gate_up, gla_b_gate=gla_b_gate,
        gla_norm_g=gla_norm_g, conv_w=conv_w, conv_b=conv_b, lru_w_a=lru_w_a, lru_b_a=lru_b_a,
        lru_w_x=lru_w_x, lru_b_x=lru_b_x, lru_lam=lru_lam, w_out_a=w_out_a, w_out_b=w_out_b,
        w_out_c=w_out_c, w_o=w_o, final_norm_g=final_norm_g))
    return (_encoder(x_prompt, meta_tokens, w), _encoder(x_sample, meta_tokens, w))
```

```python
import functools

import jax
import jax.numpy as jnp
from jax import lax
from jax.experimental import pallas as pl
from jax.experimental.pallas import tpu as pltpu

F32 = jnp.float32
BF16 = jnp.bfloat16
HIGHEST = lax.Precision.HIGHEST

N_META = 16
PAD = 48
EPS = 1e-6
S5_WIDTH = 512
S5_T = 16
S5_GB = 8
S5_NS = 512
GLA_HEADS = 4
GLA_DK = 64
GLA_DV = 128
GLA_KEY = 256
GLA_WIDTH = 512
GLA_RANK = 16
GLA_GATE_NORM = 16.0
GLA_CHUNK = 64
LRU_WIDTH = 1024
LRU_BLOCKS = 8
LRU_BLOCK = 128
CONV_WIDTH = 4
LRU_C = 8.0
HALO = 8
VMEM_LIMIT = 56 * 1024 * 1024

C1_X, C1_GC, C1_U, C1_GA, C1_V, C1_GB = 0, 1024, 2048, 2560, 3072, 3584
N1 = 4096
N2 = 640
TN = 1024
NJ1 = N1 // TN
TJS = (256, 512)


def _dot(a, b):
    return jnp.dot(a, b, preferred_element_type=F32)


def _dot_nt(a, b):
    return lax.dot_general(a, b, (((1,), (1,)), ((), ())), preferred_element_type=F32)


def _dot_nt_hi(a, b):
    return lax.dot_general(a, b, (((1,), (1,)), ((), ())), precision=HIGHEST, preferred_element_type=F32)


def _dot_tn(a, b):
    return lax.dot_general(a, b, (((0,), (0,)), ((), ())), preferred_element_type=F32)


def _sigmoid(x):
    return 0.5 * jnp.tanh(0.5 * x) + 0.5


def _params(sem):
    return pltpu.CompilerParams(dimension_semantics=sem, vmem_limit_bytes=VMEM_LIMIT)


def _kin_kernel(z_ref, g_ref, w1_ref, w2_ref, o1_ref, o2_ref, h_ref):
    j = pl.program_id(1)

    @pl.when(j == 0)
    def _():
        z = z_ref[...]
        ms = jnp.mean(z * z, axis=-1, keepdims=True)
        h_ref[...] = (z * lax.rsqrt(ms + EPS) * g_ref[0]).astype(BF16)

    @pl.when(j < NJ1)
    def _():
        o1_ref[...] = _dot(h_ref[...], w1_ref[0])

    @pl.when(j == NJ1)
    def _():
        o2_ref[...] = _dot(h_ref[...], w2_ref[0])


def _in_proj(z, g, w1, w2, layer, tm):
    r, d = z.shape
    last = NJ1 - 1
    return pl.pallas_call(
        _kin_kernel,
        grid=(r // tm, NJ1 + 1),
        in_specs=[
            pl.BlockSpec((tm, d), lambda i, j: (i, 0)),
            pl.BlockSpec((1, 1, d), lambda i, j: (layer, 0, 0)),
            pl.BlockSpec((1, d, TN), lambda i, j: (layer * NJ1 + jnp.minimum(j, last), 0, 0)),
            pl.BlockSpec((1, d, N2), lambda i, j: (layer, 0, 0)),
        ],
        out_specs=[
            pl.BlockSpec((tm, TN), lambda i, j: (i, jnp.minimum(j, last))),
            pl.BlockSpec((tm, N2), lambda i, j: (i, 0)),
        ],
        out_shape=[jax.ShapeDtypeStruct((r, N1), F32), jax.ShapeDtypeStruct((r, N2), F32)],
        scratch_shapes=[pltpu.VMEM((tm, d), BF16)],
        compiler_params=_params(("parallel", "arbitrary")),
        name="in_proj",
    )(z, g, w1, w2)


def _s5_kernel(u_ref, prm_ref, lbr_ref, lbi_ref, ctr_ref, cti_ref, dsk_ref, o_ref,
               wst_ref, m_ref, wout_ref, e_ref, w_ref, s_ref, a16_ref, *, nc):
    b = pl.program_id(1)
    ns = S5_NS
    blk = S5_GB * 16

    @pl.when(b == 0)
    def _build():
        prm = prm_ref[0, 0]
        evec = lax.broadcasted_iota(jnp.int32, (24, 1), 0).astype(F32)
        e_ref[...] = jnp.zeros_like(e_ref)
        for d in range(2):
            lr = prm[3 * d:3 * d + 1]
            li = prm[3 * d + 1:3 * d + 2]
            dt = jnp.exp(prm[3 * d + 2:3 * d + 3])
            mag = jnp.exp(evec * (lr * dt))
            ang = evec * (li * dt)
            pre = mag * jnp.cos(ang)
            pim = mag * jnp.sin(ang)
            abr = pre[1:2]
            abi = pim[1:2]
            den = lr * lr + li * li
            fr = ((abr - 1.0) * lr + abi * li) / den
            fi = (abi * lr - (abr - 1.0) * li) / den
            lbr = lbr_ref[0, 0]
            lbi = lbi_ref[0, 0]
            bbr = fr * lbr - fi * lbi
            bbi = fr * lbi + fi * lbr
            ctr = ctr_ref[0, d, 0]
            cti = cti_ref[0, d, 0]
            a16_ref[2 * d:2 * d + 1, :] = pre[S5_T:S5_T + 1]
            a16_ref[2 * d + 1:2 * d + 2, :] = pim[S5_T:S5_T + 1]
            for r in range(S5_T):
                e = S5_T - 1 - r if d == 0 else r
                pr_e = pre[e:e + 1]
                pi_e = pim[e:e + 1]
                l_re = bbr * pr_e - bbi * pi_e
                l_im = bbr * pi_e + bbi * pr_e
                base = 2 * d * ns
                wst_ref[r * blk:(r + 1) * blk, base:base + ns] = l_re.astype(BF16)
                wst_ref[r * blk:(r + 1) * blk, base + ns:base + 2 * ns] = l_im.astype(BF16)
                kblk = _dot_nt(l_re.astype(BF16), ctr.astype(BF16)) - _dot_nt(l_im.astype(BF16), cti.astype(BF16))
                eb = r if d == 0 else S5_T - 1 + r
                e_ref[eb * blk:(eb + 1) * blk, :] += kblk
                eo = r + 1 if d == 0 else S5_T - r
                pr_o = pre[eo:eo + 1]
                pi_o = pim[eo:eo + 1]
                wout_ref[r * blk:(r + 1) * blk, base:base + ns] = (pr_o * ctr - pi_o * cti).astype(BF16)
                wout_ref[r * blk:(r + 1) * blk, base + ns:base + 2 * ns] = (
                    -(pi_o * ctr) - pr_o * cti).astype(BF16)
        for t in range(S5_T):
            off = (S5_T - 1 - t) * blk
            m_ref[:, t * blk:(t + 1) * blk] = e_ref[off:off + S5_T * blk, :].astype(BF16)

    x = jnp.concatenate([u_ref[0, pl.ds(t, nc, stride=S5_T), :] for t in range(S5_T)], axis=1)
    xb = x.astype(BF16)
    w_ref[0:nc, :] = _dot(xb, wst_ref[...])

    a16 = a16_ref[0:4, :]
    far, fai, bar, bai = a16[0:1], a16[1:2], a16[2:3], a16[3:4]
    zero = jnp.zeros((1, ns), F32)

    def body(c, carry):
        fr_, fi_, br_, bi_ = carry
        cb = nc - 1 - c
        s_ref[pl.ds(c, 1), 0:ns] = fr_
        s_ref[pl.ds(c, 1), ns:2 * ns] = fi_
        s_ref[pl.ds(cb, 1), 2 * ns:3 * ns] = br_
        s_ref[pl.ds(cb, 1), 3 * ns:4 * ns] = bi_
        wfr = w_ref[pl.ds(c, 1), 0:ns]
        wfi = w_ref[pl.ds(c, 1), ns:2 * ns]
        wbr = w_ref[pl.ds(cb, 1), 2 * ns:3 * ns]
        wbi = w_ref[pl.ds(cb, 1), 3 * ns:4 * ns]
        return (far * fr_ - fai * fi_ + wfr, far * fi_ + fai * fr_ + wfi,
                bar * br_ - bai * bi_ + wbr, bar * bi_ + bai * br_ + wbi)

    lax.fori_loop(0, nc, body, (zero, zero, zero, zero))

    y = (_dot(xb, m_ref[...]) + _dot_nt(s_ref[0:nc, :].astype(BF16), wout_ref[...])
         + x * jnp.concatenate([dsk_ref[0, 0]] * S5_T, axis=1))
    for t in range(S5_T):
        o_ref[0, pl.ds(t, nc, stride=S5_T), :] = y[:, t * blk:(t + 1) * blk]


def _s5(cols1, prm, lbr, lbi, ctr, cti, dsk, layer):
    bsz, lq, _ = cols1.shape
    nc = lq // S5_T
    ncp = -(-nc // 8) * 8
    ng = S5_WIDTH // 128
    n = S5_T * 128
    ublk = C1_U // 128
    return pl.pallas_call(
        functools.partial(_s5_kernel, nc=nc),
        grid=(ng, bsz),
        in_specs=[
            pl.BlockSpec((1, lq, 128), lambda g, b: (b, 0, ublk + g)),
            pl.BlockSpec((1, 1, 8, S5_NS), lambda g, b: (layer, g, 0, 0)),
            pl.BlockSpec((1, 1, 128, S5_NS), lambda g, b: (layer, g, 0, 0)),
            pl.BlockSpec((1, 1, 128, S5_NS), lambda g, b: (layer, g, 0, 0)),
            pl.BlockSpec((1, 2, 1, 128, S5_NS), lambda g, b: (layer, 0, g, 0, 0)),
            pl.BlockSpec((1, 2, 1, 128, S5_NS), lambda g, b: (layer, 0, g, 0, 0)),
            pl.BlockSpec((1, 1, 1, 128), lambda g, b: (layer, g, 0, 0)),
        ],
        out_specs=pl.BlockSpec((1, lq, 128), lambda g, b: (b, 0, g)),
        out_shape=jax.ShapeDtypeStruct((bsz, lq, S5_WIDTH), F32),
        scratch_shapes=[
            pltpu.VMEM((n, 4 * S5_NS), BF16),
            pltpu.VMEM((n, n), BF16),
            pltpu.VMEM((n, 4 * S5_NS), BF16),
            pltpu.VMEM(((2 * S5_T - 1) * 128, 128), F32),
            pltpu.VMEM((ncp, 4 * S5_NS), F32),
            pltpu.VMEM((ncp, 4 * S5_NS), F32),
            pltpu.VMEM((8, S5_NS), F32),
        ],
        compiler_params=_params(("arbitrary", "arbitrary")),
        name="s5_mixer",
    )(cols1, prm, lbr, lbi, ctr, cti, dsk)


def _gla_kernel(*refs, rev, cpt, nt):
    if rev:
        (q_ref, k_ref, v_ref, glr_ref, wg_ref, bg_ref, of_ref, gate_ref, ng_ref, o_ref, st_ref) = refs
    else:
        (q_ref, k_ref, v_ref, glr_ref, wg_ref, bg_ref, o_ref, st_ref) = refs
    i = pl.program_id(1)
    tile = nt - 1 - i if rev else i
    ch = GLA_CHUNK

    @pl.when(i == 0)
    def _():
        st_ref[...] = jnp.zeros_like(st_ref)

    tt = cpt * ch
    row = lax.broadcasted_iota(jnp.int32, (ch, ch), 0)
    col = lax.broadcasted_iota(jnp.int32, (ch, ch), 1)
    causal = (col >= row) if rev else (col <= row)
    tri = jnp.where(causal, 1.0, 0.0).astype(F32)

    glin = _dot(glr_ref[0].astype(BF16), wg_ref[0]) + bg_ref[0]
    g = jax.nn.log_sigmoid(glin) / GLA_GATE_NORM
    t_glob = tile * tt + lax.broadcasted_iota(jnp.int32, (tt, 1), 0)
    g = jnp.where(t_glob >= PAD, g, 0.0)
    bcs = [jnp.dot(tri, g[c * ch:(c + 1) * ch], precision=HIGHEST, preferred_element_type=F32)
           for c in range(cpt)]
    blasts = [bc[0:1] if rev else bc[ch - 1:ch] for bc in bcs]
    bcum = jnp.concatenate(bcs, axis=0)
    brel = jnp.concatenate([blasts[c] - bcs[c] for c in range(cpt)], axis=0)
    k = k_ref[0]
    qe = (q_ref[0] * (GLA_DK ** -0.5) * jnp.exp(bcum)).astype(BF16)
    ke = (k * jnp.exp(-bcum)).astype(BF16)
    kd = (k * jnp.exp(brel)).astype(BF16)
    v = v_ref[0].astype(BF16)

    states = [st_ref[h] for h in range(GLA_HEADS)]
    for ci in range(cpt):
        c = cpt - 1 - ci if rev else ci
        rs = slice(c * ch, (c + 1) * ch)
        dec = jnp.exp(blasts[c])
        outs = []
        for h in range(GLA_HEADS):
            ks = slice(h * GLA_DK, (h + 1) * GLA_DK)
            vs = slice(h * GLA_DV, (h + 1) * GLA_DV)
            qh = qe[rs, ks]
            att = jnp.where(causal, _dot_nt(qh, ke[rs, ks]), 0.0)
            outs.append(_dot(att.astype(BF16), v[rs, vs]) + _dot_nt(qh, states[h].astype(BF16)))
            states[h] = states[h] * dec[:, ks] + _dot_tn(v[rs, vs], kd[rs, ks])
        o = jnp.concatenate(outs, axis=1)
        if rev:
            o = o + of_ref[0, rs, :]
            parts = []
            for h in range(GLA_HEADS):
                oh = o[:, h * GLA_DV:(h + 1) * GLA_DV]
                parts.append(oh * lax.rsqrt(jnp.mean(oh * oh, axis=-1, keepdims=True) + EPS))
            o = jnp.concatenate(parts, axis=1) * ng_ref[0]
            gate = gate_ref[0, rs, :]
            o = o * (gate * _sigmoid(gate))
        o_ref[0, rs, :] = o.astype(o_ref.dtype)
    for h in range(GLA_HEADS):
        st_ref[h] = states[h]


def _gla_pass(cols1, cols2, wg, bg, layer, rev, cpt, o_fwd=None, norm_g=None):
    bsz, lq, _ = cols1.shape
    tt = cpt * GLA_CHUNK
    nt = lq // tt
    d = 1 if rev else 0
    tmap = (lambda i: nt - 1 - i) if rev else (lambda i: i)
    in_specs = [
        pl.BlockSpec((1, tt, GLA_KEY), lambda b, i: (b, tmap(i), 0)),
        pl.BlockSpec((1, tt, GLA_KEY), lambda b, i: (b, tmap(i), 1)),
        pl.BlockSpec((1, tt, GLA_WIDTH), lambda b, i: (b, tmap(i), C1_V // GLA_WIDTH)),
        pl.BlockSpec((1, tt, 128), lambda b, i: (b, tmap(i), 2 * GLA_KEY // 128)),
        pl.BlockSpec((1, 128, GLA_KEY), lambda b, i: (layer * 2 + d, 0, 0)),
        pl.BlockSpec((1, 1, GLA_KEY), lambda b, i: (layer * 2 + d, 0, 0)),
    ]
    args = [cols2, cols2, cols1, cols2, wg, bg]
    if rev:
        in_specs += [
            pl.BlockSpec((1, tt, GLA_WIDTH), lambda b, i: (b, tmap(i), 0)),
            pl.BlockSpec((1, tt, GLA_WIDTH), lambda b, i: (b, tmap(i), C1_GB // GLA_WIDTH)),
            pl.BlockSpec((1, 1, GLA_WIDTH), lambda b, i: (layer, 0, 0)),
        ]
        args += [o_fwd, cols1, norm_g]
    return pl.pallas_call(
        functools.partial(_gla_kernel, rev=rev, cpt=cpt, nt=nt),
        grid=(bsz, nt),
        in_specs=in_specs,
        out_specs=pl.BlockSpec((1, tt, GLA_WIDTH), lambda b, i: (b, tmap(i), 0)),
        out_shape=jax.ShapeDtypeStruct((bsz, lq, GLA_WIDTH), BF16 if rev else F32),
        scratch_shapes=[pltpu.VMEM((GLA_HEADS, GLA_DV, GLA_DK), F32)],
        compiler_params=_params(("parallel", "arbitrary")),
        name="gla_bwd" if rev else "gla_fwd",
    )(*args)


def _lru_kernel(*refs, rev, tt, nt):
    if rev:
        (x_ref, xp_ref, xn_ref, cw_ref, cb_ref, w_ref, ba_ref, bx_ref, lam_ref, hf_ref, gate_ref,
         o_ref, xe_ref, a_ref, b_ref, h_ref) = refs
    else:
        (x_ref, xp_ref, xn_ref, cw_ref, cb_ref, w_ref, ba_ref, bx_ref, lam_ref,
         o_ref, xe_ref, a_ref, b_ref, h_ref) = refs
    i = pl.program_id(1)
    tile = nt - 1 - i if rev else i

    @pl.when(i == 0)
    def _():
        h_ref[...] = jnp.zeros_like(h_ref)

    xe_ref[0:HALO, :] = jnp.where(tile == 0, 0.0, xp_ref[0])
    xe_ref[HALO:HALO + tt, :] = x_ref[0]
    xe_ref[HALO + tt:2 * HALO + tt, :] = jnp.where(tile == nt - 1, 0.0, xn_ref[0])
    left = CONV_WIDTH // 2
    cw = cw_ref[0]
    xc = cb_ref[0]
    for j in range(CONV_WIDTH):
        xc = xc + xe_ref[pl.ds(HALO - left + j, tt), :] * cw[j:j + 1]
    sp = jax.nn.softplus(-lam_ref[0])
    valid = (tile * tt + lax.broadcasted_iota(jnp.int32, (tt, 1), 0)) >= PAD
    for n in range(LRU_BLOCKS):
        sl = slice(n * LRU_BLOCK, (n + 1) * LRU_BLOCK)
        xcn = xc[:, sl]
        pre = _dot(xcn.astype(BF16), w_ref[0, n])
        r = _sigmoid(pre[:, :LRU_BLOCK] + ba_ref[0][:, sl])
        ig = _sigmoid(pre[:, LRU_BLOCK:] + bx_ref[0][:, sl])
        log_a = -LRU_C * r * sp[:, sl]
        a = jnp.exp(log_a)
        one_m_a2 = -jnp.tanh(log_a) * (a * a + 1.0)
        a_ref[n] = a
        b_ref[n] = jnp.where(valid, jnp.sqrt(one_m_a2) * (ig * xcn), 0.0)

    seg = tt // 8

    def local(j, carry):
        hs, ps = carry
        jj = seg - 1 - j if rev else j
        nh, npr = [], []
        for n in range(LRU_BLOCKS):
            av = a_ref[n, pl.ds(jj, 8, stride=seg), :]
            bv = b_ref[n, pl.ds(jj, 8, stride=seg), :]
            h = av * hs[n] + bv
            b_ref[n, pl.ds(jj, 8, stride=seg), :] = h
            nh.append(h)
            npr.append(av * ps[n])
        return tuple(nh), tuple(npr)

    zeros = tuple(jnp.zeros((8, LRU_BLOCK), F32) for _ in range(LRU_BLOCKS))
    ones = tuple(jnp.ones((8, LRU_BLOCK), F32) for _ in range(LRU_BLOCKS))
    h_end, p_end = lax.fori_loop(0, seg, local, (zeros, ones))

    cins = []
    for n in range(LRU_BLOCKS):
        c = h_ref[n]
        rows = [None] * 8
        for s in (range(7, -1, -1) if rev else range(8)):
            rows[s] = c
            c = p_end[n][s:s + 1] * c + h_end[n][s:s + 1]
        h_ref[n] = c
        cins.append(jnp.concatenate(rows, axis=0))

    def fixup(j, ps):
        jj = seg - 1 - j if rev else j
        npr = []
        for n in range(LRU_BLOCKS):
            idx = (n, pl.ds(jj, 8, stride=seg), slice(None))
            p = a_ref[idx] * ps[n]
            b_ref[idx] = b_ref[idx] + p * cins[n]
            npr.append(p)
        return tuple(npr)

    lax.fori_loop(0, seg, fixup, ones)
    for n in range(LRU_BLOCKS):
        sl = slice(n * LRU_BLOCK, (n + 1) * LRU_BLOCK)
        if rev:
            g = gate_ref[0, :, sl]
            o_ref[0, :, sl] = ((hf_ref[0, :, sl] + b_ref[n]) * (g * _sigmoid(g))).astype(o_ref.dtype)
        else:
            o_ref[0, :, sl] = b_ref[n]


def _lru_pass(cols1, cw, cb, w, ba, bx, lam, layer, rev, tt, h_fwd=None):
    bsz, lq, _ = cols1.shape
    nt = lq // tt
    d = 1 if rev else 0
    nh = lq // HALO
    th = tt // HALO
    tmap = (lambda i: nt - 1 - i) if rev else (lambda i: i)
    in_specs = [
        pl.BlockSpec((1, tt, LRU_WIDTH), lambda b, i: (b, tmap(i), 0)),
        pl.BlockSpec((1, HALO, LRU_WIDTH), lambda b, i: (b, jnp.maximum(tmap(i) * th - 1, 0), 0)),
        pl.BlockSpec((1, HALO, LRU_WIDTH), lambda b, i: (b, jnp.minimum((tmap(i) + 1) * th, nh - 1), 0)),
        pl.BlockSpec((1, 8, LRU_WIDTH), lambda b, i: (layer, 0, 0)),
        pl.BlockSpec((1, 1, LRU_WIDTH), lambda b, i: (layer, 0, 0)),
        pl.BlockSpec((1, LRU_BLOCKS, LRU_BLOCK, 2 * LRU_BLOCK), lambda b, i: (layer * 2 + d, 0, 0, 0)),
        pl.BlockSpec((1, 1, LRU_WIDTH), lambda b, i: (layer * 2 + d, 0, 0)),
        pl.BlockSpec((1, 1, LRU_WIDTH), lambda b, i: (layer * 2 + d, 0, 0)),
        pl.BlockSpec((1, 1, LRU_WIDTH), lambda b, i: (layer * 2 + d, 0, 0)),
    ]
    args = [cols1, cols1, cols1, cw, cb, w, ba, bx, lam]
    if rev:
        in_specs += [
            pl.BlockSpec((1, tt, LRU_WIDTH), lambda b, i: (b, tmap(i), 0)),
            pl.BlockSpec((1, tt, LRU_WIDTH), lambda b, i: (b, tmap(i), C1_GC // LRU_WIDTH)),
        ]
        args += [h_fwd, cols1]
    return pl.pallas_call(
        functools.partial(_lru_kernel, rev=rev, tt=tt, nt=nt),
        grid=(bsz, nt),
        in_specs=in_specs,
        out_specs=pl.BlockSpec((1, tt, LRU_WIDTH), lambda b, i: (b, tmap(i), 0)),
        out_shape=jax.ShapeDtypeStruct((bsz, lq, LRU_WIDTH), BF16 if rev else F32),
        scratch_shapes=[
            pltpu.VMEM((tt + 2 * HALO, LRU_WIDTH), F32),
            pltpu.VMEM((LRU_BLOCKS, tt, LRU_BLOCK), F32),
            pltpu.VMEM((LRU_BLOCKS, tt, LRU_BLOCK), F32),
            pltpu.VMEM((LRU_BLOCKS, 1, LRU_BLOCK), F32),
        ],
        compiler_params=_params(("parallel", "arbitrary")),
        name="lru_bwd" if rev else "lru_fwd",
    )(*args)


def _out_kernel(z_ref, g_ref, mask_ref, ys_ref, ga_ref, yb_ref, yc_ref, wglu_ref, bglu_ref,
                wm0_ref, wm1_ref, wm2_ref, woa_ref, wob_ref, woc_ref, wo_ref, fg_ref,
                o_ref, h_ref, ya_ref, *, final):
    j = pl.program_id(1)
    nj = pl.num_programs(1)

    @pl.when(j == 0)
    def _():
        z = z_ref[...]
        ms = jnp.mean(z * z, axis=-1, keepdims=True)
        h_ref[...] = (z * lax.rsqrt(ms + EPS) * g_ref[0]).astype(BF16)
        zz = jax.nn.gelu(ys_ref[...])
        glu = _sigmoid(_dot(zz.astype(BF16), wglu_ref[0]) + bglu_ref[0])
        ga = ga_ref[...]
        ya_ref[...] = (zz * glu * (ga * _sigmoid(ga))).astype(BF16)

    h = h_ref[...]
    m = (_sigmoid(_dot(h, wm0_ref[0])) * _dot(ya_ref[...], woa_ref[0])
         + _sigmoid(_dot(h, wm1_ref[0])) * _dot(yb_ref[...], wob_ref[0])
         + _sigmoid(_dot(h, wm2_ref[0])) * _dot(yc_ref[...], woc_ref[0]))
    contrib = _dot(m.astype(BF16), wo_ref[0])

    @pl.when(j == 0)
    def _():
        o_ref[...] = contrib

    @pl.when(j > 0)
    def _():
        o_ref[...] += contrib

    @pl.when(j == nj - 1)
    def _():
        zn = z_ref[...] + o_ref[...]
        if final:
            ms = jnp.mean(zn * zn, axis=-1, keepdims=True)
            o_ref[...] = zn * lax.rsqrt(ms + EPS) * fg_ref[...]
        else:
            o_ref[...] = jnp.where(mask_ref[...] > 0.0, zn, 0.0)


def _out_proj(z, g, mask, cols1, ys, yb, yc, wglu, bglu, wm, woa, wob, woc, wo, fg, layer, tm, final):
    r, d = z.shape
    TJ = wm.shape[-1]
    nj = d // TJ
    return pl.pallas_call(
        functools.partial(_out_kernel, final=final),
        grid=(r // tm, nj),
        in_specs=[
            pl.BlockSpec((tm, d), lambda i, j: (i, 0)),
            pl.BlockSpec((1, 1, d), lambda i, j: (layer, 0, 0)),
            pl.BlockSpec((tm, 1), lambda i, j: (i, 0)),
            pl.BlockSpec((tm, S5_WIDTH), lambda i, j: (i, 0)),
            pl.BlockSpec((tm, S5_WIDTH), lambda i, j: (i, C1_GA // S5_WIDTH)),
            pl.BlockSpec((tm, GLA_WIDTH), lambda i, j: (i, 0)),
            pl.BlockSpec((tm, LRU_WIDTH), lambda i, j: (i, 0)),
            pl.BlockSpec((1, S5_WIDTH, S5_WIDTH), lambda i, j: (layer, 0, 0)),
            pl.BlockSpec((1, 1, S5_WIDTH), lambda i, j: (layer, 0, 0)),
            pl.BlockSpec((1, d, TJ), lambda i, j: (layer * 3 * nj + j, 0, 0)),
            pl.BlockSpec((1, d, TJ), lambda i, j: (layer * 3 * nj + nj + j, 0, 0)),
            pl.BlockSpec((1, d, TJ), lambda i, j: (layer * 3 * nj + 2 * nj + j, 0, 0)),
            pl.BlockSpec((1, S5_WIDTH, TJ), lambda i, j: (layer * nj + j, 0, 0)),
            pl.BlockSpec((1, GLA_WIDTH, TJ), lambda i, j: (layer * nj + j, 0, 0)),
            pl.BlockSpec((1, LRU_WIDTH, TJ), lambda i, j: (layer * nj + j, 0, 0)),
            pl.BlockSpec((1, TJ, d), lambda i, j: (layer, j, 0)),
            pl.BlockSpec((1, d), lambda i, j: (0, 0)),
        ],
        out_specs=pl.BlockSpec((tm, d), lambda i, j: (i, 0)),
        out_shape=jax.ShapeDtypeStruct((r, d), F32),
        scratch_shapes=[
            pltpu.VMEM((tm, d), BF16),
            pltpu.VMEM((tm, S5_WIDTH), BF16),
        ],
        compiler_params=_params(("parallel", "arbitrary")),
        name="out_proj",
    )(z, g, mask, ys, cols1, yb, yc, wglu, bglu, wm, wm, wm, woa, wob, woc, wo, fg)


def _prepare(p):
    depth = p["w_in"].shape[0]
    w_in = p["w_in"]
    o = 0
    seg = {}
    for name, width in (("u", 512), ("ga", 512), ("q", 256), ("k", 256), ("v", 512), ("gb", 512),
                        ("glr", 2 * GLA_RANK), ("x", 1024), ("gc", 1024), ("m", 3 * w_in.shape[1])):
        seg[name] = w_in[:, :, o:o + width]
        o += width
    w1 = jnp.concatenate([seg["x"], seg["gc"], seg["u"], seg["ga"], seg["v"], seg["gb"]], axis=-1).astype(BF16)
    zpad = jnp.zeros(w_in.shape[:2] + (N2 - 2 * GLA_KEY - 2 * GLA_RANK,), w_in.dtype)
    w2 = jnp.concatenate([seg["q"], seg["k"], seg["glr"], zpad], axis=-1).astype(BF16)
    wm = seg["m"].astype(BF16)

    def col_tiles(a, width):
        dep, k, n = a.shape
        return a.reshape(dep, k, n // width, width).transpose(0, 2, 1, 3).reshape(dep * (n // width), k, width)

    w1 = col_tiles(w1, TN)
    wm = {tj: col_tiles(wm, tj) for tj in TJS}

    ng = S5_WIDTH // 128
    eye = jnp.eye(S5_GB, dtype=F32)

    def rows(a):
        return a.astype(F32).reshape(depth, 2, ng, S5_NS)

    ls = jnp.broadcast_to(p["s5_log_step"].astype(F32)[..., None], p["s5_lam_re"].shape)
    lr, li, ls = rows(p["s5_lam_re"]), rows(p["s5_lam_im"]), rows(ls)
    zero = jnp.zeros_like(lr[:, 0])
    prm = jnp.stack([lr[:, 0], li[:, 0], ls[:, 0], lr[:, 1], li[:, 1], ls[:, 1], zero, zero], axis=2)

    def emb_b(bm):
        bm = bm.astype(F32).reshape(depth, ng, S5_GB, 64, 16)
        return jnp.einsum("lGgnc,gh->lGgchn", bm, eye).reshape(depth, ng, 128, S5_NS)

    def emb_c(cm):
        cm = cm.astype(F32).reshape(depth, 2, ng, S5_GB, 16, 64)
        return jnp.einsum("ldGgcn,gh->ldGgchn", cm, eye).reshape(depth, 2, ng, 128, S5_NS)

    wg = jnp.zeros((depth, 2, 128, GLA_KEY), F32)
    for d in range(2):
        wg = wg.at[:, d, d * GLA_RANK:(d + 1) * GLA_RANK, :].set(p["gla_w_gate_up"][:, d].astype(F32))
    lru_w = jnp.concatenate([p["lru_w_a"], p["lru_w_x"]], axis=-1).astype(BF16)
    cw = jnp.concatenate([p["conv_w"].astype(F32),
                          jnp.zeros((depth, 8 - CONV_WIDTH, LRU_WIDTH), F32)], axis=1)
    return dict(
        norm_g=p["norm_g"].astype(F32).reshape(depth, 1, -1), w1=w1, w2=w2, wm=wm,
        s5_prm=prm, s5_lbr=emb_b(p["s5_b_re"]), s5_lbi=emb_b(p["s5_b_im"]),
        s5_ctr=emb_c(p["s5_c_re"]), s5_cti=emb_c(p["s5_c_im"]),
        s5_d=p["s5_d"].astype(F32).reshape(depth, ng, 1, 128),
        wglu=p["s5_w_glu"].astype(BF16), bglu=p["s5_b_glu"].astype(F32).reshape(depth, 1, S5_WIDTH),
        wg=wg.reshape(depth * 2, 128, GLA_KEY).astype(BF16),
        bg=p["gla_b_gate"].astype(F32).reshape(depth * 2, 1, GLA_KEY),
        gla_ng=p["gla_norm_g"].astype(F32).reshape(depth, 1, GLA_WIDTH),
        cw=cw, cb=p["conv_b"].astype(F32).reshape(depth, 1, LRU_WIDTH),
        lru_w=lru_w.reshape(depth * 2, LRU_BLOCKS, LRU_BLOCK, 2 * LRU_BLOCK),
        lru_ba=p["lru_b_a"].astype(F32).reshape(depth * 2, 1, LRU_WIDTH),
        lru_bx=p["lru_b_x"].astype(F32).reshape(depth * 2, 1, LRU_WIDTH),
        lru_lam=p["lru_lam"].astype(F32).reshape(depth * 2, 1, LRU_WIDTH),
        woa={tj: col_tiles(p["w_out_a"].astype(BF16), tj) for tj in TJS},
        wob={tj: col_tiles(p["w_out_b"].astype(BF16), tj) for tj in TJS},
        woc={tj: col_tiles(p["w_out_c"].astype(BF16), tj) for tj in TJS},
        wo=p["w_o"].astype(BF16), fg=p["final_norm_g"].astype(F32).reshape(1, -1),
    )


def _pick_tile(n, unit, target):
    best = unit
    for t in range(unit, target + 1, unit):
        if n % t == 0:
            best = t
    return best


def _pick_lru_tile(lq, target):
    best = None
    for seg in range(1, target // 8 + 1):
        if lq % (8 * seg) == 0 and seg % 8 != 0:
            best = 8 * seg
    assert best is not None
    return best


def _encoder(x, meta, w):
    bsz, seq, d = x.shape
    depth = w["w2"].shape[0]
    lq = PAD + N_META + seq
    r = bsz * lq
    head = jnp.concatenate([jnp.zeros((PAD, d), F32), meta.astype(F32)], axis=0)
    z = jnp.concatenate([jnp.broadcast_to(head[None], (bsz, PAD + N_META, d)), x.astype(F32)], axis=1)
    z = z.reshape(r, d)
    mask = jnp.broadcast_to((jnp.arange(lq) >= PAD).astype(F32)[None, :, None], (bsz, lq, 1)).reshape(r, 1)
    tm = _pick_tile(r, 128, 768)
    tm_out = _pick_tile(r, 128, 640)
    tj = TJS[1] if tm_out <= 512 else TJS[0]
    cpt = _pick_tile(lq // GLA_CHUNK, 1, 8)
    tt = _pick_lru_tile(lq, 640)
    for l in range(depth):
        cols1, cols2 = _in_proj(z, w["norm_g"], w["w1"], w["w2"], l, tm)
        c1 = cols1.reshape(bsz, lq, N1)
        c2 = cols2.reshape(bsz, lq, N2)
        ys = _s5(c1, w["s5_prm"], w["s5_lbr"], w["s5_lbi"], w["s5_ctr"], w["s5_cti"], w["s5_d"], l)
        of = _gla_pass(c1, c2, w["wg"], w["bg"], l, False, cpt)
        yb = _gla_pass(c1, c2, w["wg"], w["bg"], l, True, cpt, of, w["gla_ng"])
        hf = _lru_pass(c1, w["cw"], w["cb"], w["lru_w"], w["lru_ba"], w["lru_bx"], w["lru_lam"], l, False, tt)
        yc = _lru_pass(c1, w["cw"], w["cb"], w["lru_w"], w["lru_ba"], w["lru_bx"], w["lru_lam"], l, True, tt, hf)
        z = _out_proj(z, w["norm_g"], mask, cols1, ys.reshape(r, -1), yb.reshape(r, -1),
                      yc.reshape(r, -1), w["wglu"], w["bglu"], w["wm"][tj], w["woa"][tj], w["wob"][tj], w["woc"][tj],
                      w["wo"], w["fg"], l, tm_out, l == depth - 1)
    return z.reshape(bsz, lq, d)[:, PAD + N_META:].astype(x.dtype)


def kernel(x_prompt, x_sample, meta_tokens, norm_g, w_in, s5_lam_re, s5_lam_im, s5_log_step, s5_b_re, s5_b_im, s5_c_re, s5_c_im, s5_d, s5_w_glu, s5_b_glu, gla_w_gate_up, gla_b_gate, gla_norm_g, conv_w, conv_b, lru_w_a, lru_b_a, lru_w_x, lru_b_x, lru_lam, w_out_a, w_out_b, w_out_c, w_o, final_norm_g):
    w = _prepare(dict(
        norm_g=norm_g, w_in=w_in, s5_lam_re=s5_lam_re, s5_lam_im=s5_lam_im, s5_log_step=s5_log_step,
        s5_b_re=s5_b_re, s5_b_im=s5_b_im, s5_c_re=s5_c_re, s5_c_im=s5_c_im, s5_d=s5_d,
        s5_w_glu=s5_w_glu, s5_b_glu=s5_b_glu, gla_w_gate_up=gla_w_gate_up, gla_b_gate=gla_b_gate,
        gla_norm_g=gla_norm_g, conv_w=conv_w, conv_b=conv_b, lru_w_a=lru_w_a, lru_b_a=lru_b_a,
        lru_w_x=lru_w_x, lru_b_x=lru_b_x, lru_lam=lru_lam, w_out_a=w_out_a, w_out_b=w_out_b,
        w_out_c=w_out_c, w_o=w_o, final_norm_g=final_norm_g))
    return (_encoder(x_prompt, meta_tokens, w), _encoder(x_sample, meta_tokens, w))
```

```python
import functools

import jax
import jax.numpy as jnp
from jax import lax
from jax.experimental import pallas as pl
from jax.experimental.pallas import tpu as pltpu

F32 = jnp.float32
BF16 = jnp.bfloat16
HIGHEST = lax.Precision.HIGHEST

N_META = 16
PAD = 48
EPS = 1e-6
S5_WIDTH = 512
S5_T = 16
S5_GB = 8
S5_NS = 512
GLA_HEADS = 4
GLA_DK = 64
GLA_DV = 128
GLA_KEY = 256
GLA_WIDTH = 512
GLA_RANK = 16
GLA_GATE_NORM = 16.0
GLA_CHUNK = 64
LRU_WIDTH = 1024
LRU_BLOCKS = 8
LRU_BLOCK = 128
CONV_WIDTH = 4
LRU_C = 8.0
HALO = 8
VMEM_LIMIT = 56 * 1024 * 1024

C1_X, C1_GC, C1_U, C1_GA, C1_V, C1_GB = 0, 1024, 2048, 2560, 3072, 3584
N1 = 4096
N2 = 640
TN = 1024
NJ1 = N1 // TN
TJS = (256, 512)


def _dot(a, b):
    return jnp.dot(a, b, preferred_element_type=F32)


def _dot_nt(a, b):
    return lax.dot_general(a, b, (((1,), (1,)), ((), ())), preferred_element_type=F32)


def _dot_nt_hi(a, b):
    return lax.dot_general(a, b, (((1,), (1,)), ((), ())), precision=HIGHEST, preferred_element_type=F32)


def _dot_tn(a, b):
    return lax.dot_general(a, b, (((0,), (0,)), ((), ())), preferred_element_type=F32)


def _sigmoid(x):
    return 0.5 * jnp.tanh(0.5 * x) + 0.5


def _params(sem):
    return pltpu.CompilerParams(dimension_semantics=sem, vmem_limit_bytes=VMEM_LIMIT)


def _kin_kernel(z_ref, g_ref, w1_ref, w2_ref, o1_ref, o2_ref, h_ref):
    j = pl.program_id(1)

    @pl.when(j == 0)
    def _():
        z = z_ref[...]
        ms = jnp.mean(z * z, axis=-1, keepdims=True)
        h_ref[...] = (z * lax.rsqrt(ms + EPS) * g_ref[0]).astype(BF16)

    @pl.when(j < NJ1)
    def _():
        o1_ref[...] = _dot(h_ref[...], w1_ref[0])

    @pl.when(j == NJ1)
    def _():
        o2_ref[...] = _dot(h_ref[...], w2_ref[0])


def _in_proj(z, g, w1, w2, layer, tm):
    r, d = z.shape
    last = NJ1 - 1
    return pl.pallas_call(
        _kin_kernel,
        grid=(r // tm, NJ1 + 1),
        in_specs=[
            pl.BlockSpec((tm, d), lambda i, j: (i, 0)),
            pl.BlockSpec((1, 1, d), lambda i, j: (layer, 0, 0)),
            pl.BlockSpec((1, d, TN), lambda i, j: (layer, 0, jnp.minimum(j, last))),
            pl.BlockSpec((1, d, N2), lambda i, j: (layer, 0, 0)),
        ],
        out_specs=[
            pl.BlockSpec((tm, TN), lambda i, j: (i, jnp.minimum(j, last))),
            pl.BlockSpec((tm, N2), lambda i, j: (i, 0)),
        ],
        out_shape=[jax.ShapeDtypeStruct((r, N1), F32), jax.ShapeDtypeStruct((r, N2), F32)],
        scratch_shapes=[pltpu.VMEM((tm, d), BF16)],
        compiler_params=_params(("parallel", "arbitrary")),
        name="in_proj",
    )(z, g, w1, w2)


def _s5_kernel(u_ref, prm_ref, lbr_ref, lbi_ref, ctr_ref, cti_ref, dsk_ref, o_ref,
               wst_ref, m_ref, wout_ref, e_ref, w_ref, s_ref, a16_ref, *, nc):
    b = pl.program_id(1)
    ns = S5_NS
    blk = S5_GB * 16
    ncp = -(-nc // 8) * 8

    @pl.when(b == 0)
    def _build():
        prm = prm_ref[0, 0]
        evec = lax.broadcasted_iota(jnp.int32, (24, 1), 0).astype(F32)
        e_ref[...] = jnp.zeros_like(e_ref)
        for d in range(2):
            lr = prm[3 * d:3 * d + 1]
            li = prm[3 * d + 1:3 * d + 2]
            dt = jnp.exp(prm[3 * d + 2:3 * d + 3])
            mag = jnp.exp(evec * (lr * dt))
            ang = evec * (li * dt)
            pre = mag * jnp.cos(ang)
            pim = mag * jnp.sin(ang)
            abr = pre[1:2]
            abi = pim[1:2]
            den = lr * lr + li * li
            fr = ((abr - 1.0) * lr + abi * li) / den
            fi = (abi * lr - (abr - 1.0) * li) / den
            lbr = lbr_ref[0, 0]
            lbi = lbi_ref[0, 0]
            bbr = fr * lbr - fi * lbi
            bbi = fr * lbi + fi * lbr
            ctr = ctr_ref[0, d, 0]
            cti = cti_ref[0, d, 0]
            a16_ref[2 * d:2 * d + 1, :] = pre[S5_T:S5_T + 1]
            a16_ref[2 * d + 1:2 * d + 2, :] = pim[S5_T:S5_T + 1]
            sr, si = jnp.ones_like(lr), jnp.zeros_like(lr)
            br_, bi_ = pre[S5_T:S5_T + 1], pim[S5_T:S5_T + 1]
            nseg = ncp // 8
            while nseg:
                if nseg & 1:
                    sr, si = sr * br_ - si * bi_, sr * bi_ + si * br_
                br_, bi_ = br_ * br_ - bi_ * bi_, 2.0 * br_ * bi_
                nseg >>= 1
            a16_ref[4 + 2 * d:5 + 2 * d, :] = sr
            a16_ref[5 + 2 * d:6 + 2 * d, :] = si
            for r in range(S5_T):
                e = S5_T - 1 - r if d == 0 else r
                pr_e = pre[e:e + 1]
                pi_e = pim[e:e + 1]
                l_re = bbr * pr_e - bbi * pi_e
                l_im = bbr * pi_e + bbi * pr_e
                base = 2 * d * ns
                wst_ref[r * blk:(r + 1) * blk, base:base + ns] = l_re.astype(BF16)
                wst_ref[r * blk:(r + 1) * blk, base + ns:base + 2 * ns] = l_im.astype(BF16)
                kblk = _dot_nt(l_re.astype(BF16), ctr.astype(BF16)) - _dot_nt(l_im.astype(BF16), cti.astype(BF16))
                eb = r if d == 0 else S5_T - 1 + r
                e_ref[eb * blk:(eb + 1) * blk, :] += kblk
                eo = r + 1 if d == 0 else S5_T - r
                pr_o = pre[eo:eo + 1]
                pi_o = pim[eo:eo + 1]
                wout_ref[r * blk:(r + 1) * blk, base:base + ns] = (pr_o * ctr - pi_o * cti).astype(BF16)
                wout_ref[r * blk:(r + 1) * blk, base + ns:base + 2 * ns] = (
                    -(pi_o * ctr) - pr_o * cti).astype(BF16)
        for t in range(S5_T):
            off = (S5_T - 1 - t) * blk
            m_ref[:, t * blk:(t + 1) * blk] = e_ref[off:off + S5_T * blk, :].astype(BF16)

    x = jnp.concatenate([u_ref[0, pl.ds(t, nc, stride=S5_T), :] for t in range(S5_T)], axis=1)
    xb = x.astype(BF16)
    wv = _dot(xb, wst_ref[...])
    nq = ns // 128
    for kk in range(4 * nq):
        w_ref[kk, 0:nc, :] = wv[:, kk * 128:(kk + 1) * 128]
        if ncp > nc:
            w_ref[kk, nc:ncp, :] = jnp.zeros((ncp - nc, 128), F32)

    seg = ncp // 8
    apow = a16_ref[...]
    pairs = [(d, q) for d in range(2) for q in range(nq)]

    def coef(row, q):
        return apow[row:row + 1, q * 128:(q + 1) * 128]

    def rows(d, q, j):
        jj = j if d == 0 else seg - 1 - j
        return ((2 * d * nq + q, pl.ds(jj, 8, stride=seg), slice(None)),
                ((2 * d + 1) * nq + q, pl.ds(jj, 8, stride=seg), slice(None)))

    def local(j, carry):
        new = []
        for n, (d, q) in enumerate(pairs):
            cr, ci = carry[2 * n], carry[2 * n + 1]
            ir, ii = rows(d, q, j)
            ar, ai = coef(2 * d, q), coef(2 * d + 1, q)
            s_ref[ir] = cr
            s_ref[ii] = ci
            new += [ar * cr - ai * ci + w_ref[ir], ar * ci + ai * cr + w_ref[ii]]
        return tuple(new)

    zero8 = jnp.zeros((8, 128), F32)
    totals = lax.fori_loop(0, seg, local, tuple(zero8 for _ in range(2 * len(pairs))))

    cins = []
    for n, (d, q) in enumerate(pairs):
        tr, ti = totals[2 * n], totals[2 * n + 1]
        sr, si = coef(4 + 2 * d, q), coef(5 + 2 * d, q)
        cr = jnp.zeros((1, 128), F32)
        ci = jnp.zeros((1, 128), F32)
        rr, ri = [None] * 8, [None] * 8
        for s in (range(8) if d == 0 else range(7, -1, -1)):
            rr[s], ri[s] = cr, ci
            cr, ci = sr * cr - si * ci + tr[s:s + 1], sr * ci + si * cr + ti[s:s + 1]
        cins += [jnp.concatenate(rr, axis=0), jnp.concatenate(ri, axis=0)]

    def fixup(j, ps):
        new = []
        for n, (d, q) in enumerate(pairs):
            pr, pi = ps[2 * n], ps[2 * n + 1]
            ir, ii = rows(d, q, j)
            cr, ci = cins[2 * n], cins[2 * n + 1]
            s_ref[ir] = s_ref[ir] + (pr * cr - pi * ci)
            s_ref[ii] = s_ref[ii] + (pr * ci + pi * cr)
            ar, ai = coef(2 * d, q), coef(2 * d + 1, q)
            new += [ar * pr - ai * pi, ar * pi + ai * pr]
        return tuple(new)

    one8 = jnp.ones((8, 128), F32)
    lax.fori_loop(0, seg, fixup, tuple(one8 if n % 2 == 0 else zero8 for n in range(2 * len(pairs))))

    s_in = jnp.concatenate([s_ref[kk, 0:nc, :] for kk in range(4 * nq)], axis=1).astype(BF16)
    y = (_dot(xb, m_ref[...]) + _dot_nt(s_in, wout_ref[...])
         + x * jnp.concatenate([dsk_ref[0, 0]] * S5_T, axis=1))
    for t in range(S5_T):
        o_ref[0, pl.ds(t, nc, stride=S5_T), :] = y[:, t * blk:(t + 1) * blk]


def _s5(cols1, prm, lbr, lbi, ctr, cti, dsk, layer):
    bsz, lq, _ = cols1.shape
    nc = lq // S5_T
    ncp = -(-nc // 8) * 8
    ng = S5_WIDTH // 128
    n = S5_T * 128
    ublk = C1_U // 128
    return pl.pallas_call(
        functools.partial(_s5_kernel, nc=nc),
        grid=(ng, bsz),
        in_specs=[
            pl.BlockSpec((1, lq, 128), lambda g, b: (b, 0, ublk + g)),
            pl.BlockSpec((1, 1, 8, S5_NS), lambda g, b: (layer, g, 0, 0)),
            pl.BlockSpec((1, 1, 128, S5_NS), lambda g, b: (layer, g, 0, 0)),
            pl.BlockSpec((1, 1, 128, S5_NS), lambda g, b: (layer, g, 0, 0)),
            pl.BlockSpec((1, 2, 1, 128, S5_NS), lambda g, b: (layer, 0, g, 0, 0)),
            pl.BlockSpec((1, 2, 1, 128, S5_NS), lambda g, b: (layer, 0, g, 0, 0)),
            pl.BlockSpec((1, 1, 1, 128), lambda g, b: (layer, g, 0, 0)),
        ],
        out_specs=pl.BlockSpec((1, lq, 128), lambda g, b: (b, 0, g)),
        out_shape=jax.ShapeDtypeStruct((bsz, lq, S5_WIDTH), F32),
        scratch_shapes=[
            pltpu.VMEM((n, 4 * S5_NS), BF16),
            pltpu.VMEM((n, n), BF16),
            pltpu.VMEM((n, 4 * S5_NS), BF16),
            pltpu.VMEM(((2 * S5_T - 1) * 128, 128), F32),
            pltpu.VMEM((4 * S5_NS // 128, ncp, 128), F32),
            pltpu.VMEM((4 * S5_NS // 128, ncp, 128), F32),
            pltpu.VMEM((8, S5_NS), F32),
        ],
        compiler_params=_params(("arbitrary", "arbitrary")),
        name="s5_mixer",
    )(cols1, prm, lbr, lbi, ctr, cti, dsk)


def _gla_kernel(*refs, rev, cpt, nt):
    if rev:
        (q_ref, k_ref, v_ref, glr_ref, wg_ref, bg_ref, of_ref, gate_ref, ng_ref, o_ref, st_ref) = refs
    else:
        (q_ref, k_ref, v_ref, glr_ref, wg_ref, bg_ref, o_ref, st_ref) = refs
    i = pl.program_id(1)
    tile = nt - 1 - i if rev else i
    ch = GLA_CHUNK

    @pl.when(i == 0)
    def _():
        st_ref[...] = jnp.zeros_like(st_ref)

    tt = cpt * ch
    row = lax.broadcasted_iota(jnp.int32, (ch, ch), 0)
    col = lax.broadcasted_iota(jnp.int32, (ch, ch), 1)
    causal = (col >= row) if rev else (col <= row)
    tri = jnp.where(causal, 1.0, 0.0).astype(F32)

    glin = _dot(glr_ref[0].astype(BF16), wg_ref[0]) + bg_ref[0]
    g = jax.nn.log_sigmoid(glin) / GLA_GATE_NORM
    t_glob = tile * tt + lax.broadcasted_iota(jnp.int32, (tt, 1), 0)
    g = jnp.where(t_glob >= PAD, g, 0.0)
    bcs = [jnp.dot(tri, g[c * ch:(c + 1) * ch], precision=HIGHEST, preferred_element_type=F32)
           for c in range(cpt)]
    blasts = [bc[0:1] if rev else bc[ch - 1:ch] for bc in bcs]
    bcum = jnp.concatenate(bcs, axis=0)
    brel = jnp.concatenate([blasts[c] - bcs[c] for c in range(cpt)], axis=0)
    k = k_ref[0]
    qe = (q_ref[0] * (GLA_DK ** -0.5) * jnp.exp(bcum)).astype(BF16)
    ke = (k * jnp.exp(-bcum)).astype(BF16)
    kd = (k * jnp.exp(brel)).astype(BF16)
    v = v_ref[0].astype(BF16)

    states = [st_ref[h] for h in range(GLA_HEADS)]
    for ci in range(cpt):
        c = cpt - 1 - ci if rev else ci
        rs = slice(c * ch, (c + 1) * ch)
        dec = jnp.exp(blasts[c])
        outs = []
        for h in range(GLA_HEADS):
            ks = slice(h * GLA_DK, (h + 1) * GLA_DK)
            vs = slice(h * GLA_DV, (h + 1) * GLA_DV)
            qh = qe[rs, ks]
            att = jnp.where(causal, _dot_nt(qh, ke[rs, ks]), 0.0)
            outs.append(_dot(att.astype(BF16), v[rs, vs]) + _dot_nt(qh, states[h].astype(BF16)))
            states[h] = states[h] * dec[:, ks] + _dot_tn(v[rs, vs], kd[rs, ks])
        o = jnp.concatenate(outs, axis=1)
        if rev:
            o = o + of_ref[0, rs, :]
            parts = []
            for h in range(GLA_HEADS):
                oh = o[:, h * GLA_DV:(h + 1) * GLA_DV]
                parts.append(oh * lax.rsqrt(jnp.mean(oh * oh, axis=-1, keepdims=True) + EPS))
            o = jnp.concatenate(parts, axis=1) * ng_ref[0]
            gate = gate_ref[0, rs, :]
            o = o * (gate * _sigmoid(gate))
        o_ref[0, rs, :] = o.astype(o_ref.dtype)
    for h in range(GLA_HEADS):
        st_ref[h] = states[h]


def _gla_pass(cols1, cols2, wg, bg, layer, rev, cpt, o_fwd=None, norm_g=None):
    bsz, lq, _ = cols1.shape
    tt = cpt * GLA_CHUNK
    nt = lq // tt
    d = 1 if rev else 0
    tmap = (lambda i: nt - 1 - i) if rev else (lambda i: i)
    in_specs = [
        pl.BlockSpec((1, tt, GLA_KEY), lambda b, i: (b, tmap(i), 0)),
        pl.BlockSpec((1, tt, GLA_KEY), lambda b, i: (b, tmap(i), 1)),
        pl.BlockSpec((1, tt, GLA_WIDTH), lambda b, i: (b, tmap(i), C1_V // GLA_WIDTH)),
        pl.BlockSpec((1, tt, 128), lambda b, i: (b, tmap(i), 2 * GLA_KEY // 128)),
        pl.BlockSpec((1, 128, GLA_KEY), lambda b, i: (layer * 2 + d, 0, 0)),
        pl.BlockSpec((1, 1, GLA_KEY), lambda b, i: (layer * 2 + d, 0, 0)),
    ]
    args = [cols2, cols2, cols1, cols2, wg, bg]
    if rev:
        in_specs += [
            pl.BlockSpec((1, tt, GLA_WIDTH), lambda b, i: (b, tmap(i), 0)),
            pl.BlockSpec((1, tt, GLA_WIDTH), lambda b, i: (b, tmap(i), C1_GB // GLA_WIDTH)),
            pl.BlockSpec((1, 1, GLA_WIDTH), lambda b, i: (layer, 0, 0)),
        ]
        args += [o_fwd, cols1, norm_g]
    return pl.pallas_call(
        functools.partial(_gla_kernel, rev=rev, cpt=cpt, nt=nt),
        grid=(bsz, nt),
        in_specs=in_specs,
        out_specs=pl.BlockSpec((1, tt, GLA_WIDTH), lambda b, i: (b, tmap(i), 0)),
        out_shape=jax.ShapeDtypeStruct((bsz, lq, GLA_WIDTH), BF16 if rev else F32),
        scratch_shapes=[pltpu.VMEM((GLA_HEADS, GLA_DV, GLA_DK), F32)],
        compiler_params=_params(("parallel", "arbitrary")),
        name="gla_bwd" if rev else "gla_fwd",
    )(*args)


def _lru_kernel(*refs, rev, tt, nt):
    xs = refs[:LRU_BLOCKS]
    refs = refs[LRU_BLOCKS:]
    if rev:
        (xp_ref, xn_ref, cw_ref, cb_ref, w_ref, ba_ref, bx_ref, lam_ref, hf_ref) = refs[:9]
        gates = refs[9:9 + LRU_BLOCKS]
        o_ref, a_ref, b_ref, h_ref, on_ref = refs[9 + LRU_BLOCKS:]
    else:
        (xp_ref, xn_ref, cw_ref, cb_ref, w_ref, ba_ref, bx_ref, lam_ref,
         o_ref, a_ref, b_ref, h_ref) = refs
    i = pl.program_id(1)
    tile = nt - 1 - i if rev else i
    seg = tt // 8

    @pl.when(i == 0)
    def _():
        h_ref[...] = jnp.zeros_like(h_ref)

    def permuted(ref, lead):
        return jnp.concatenate([ref[lead + (pl.ds(j, 8, stride=seg), slice(None))] for j in range(seg)], axis=0)

    prev = jnp.where(tile == 0, 0.0, xp_ref[0])
    nxt = jnp.where(tile == nt - 1, 0.0, xn_ref[0])
    sub = lax.broadcasted_iota(jnp.int32, (8, LRU_BLOCK), 0)
    left = CONV_WIDTH // 2
    cw = cw_ref[0]
    sp = jax.nn.softplus(-lam_ref[0])
    rperm = lax.broadcasted_iota(jnp.int32, (tt, 1), 0)
    valid = (tile * tt + (rperm & 7) * seg + (rperm >> 3)) >= PAD
    for n in range(LRU_BLOCKS):
        sl = slice(n * LRU_BLOCK, (n + 1) * LRU_BLOCK)
        xp = permuted(xs[n], (0,))
        before = []
        for k in range(left, 0, -1):
            edge = pltpu.roll(xp[(seg - k) * 8:(seg - k + 1) * 8], 1, axis=0)
            before.append(jnp.where(sub == 0, prev[HALO - k:HALO - k + 1, sl], edge))
        after = []
        for k in range(CONV_WIDTH - 1 - left):
            edge = pltpu.roll(xp[k * 8:(k + 1) * 8], 7, axis=0)
            after.append(jnp.where(sub == 7, nxt[k:k + 1, sl], edge))
        xext = jnp.concatenate(before + [xp] + after, axis=0)
        xcn = cb_ref[0][:, sl]
        for j in range(CONV_WIDTH):
            xcn = xcn + xext[8 * j:8 * j + tt] * cw[j:j + 1, sl]
        pre = _dot(xcn.astype(BF16), w_ref[0, n])
        tr = jnp.tanh(pre[:, :LRU_BLOCK] + 0.5 * ba_ref[0][:, sl])
        ig = 0.5 * jnp.tanh(pre[:, LRU_BLOCK:] + 0.5 * bx_ref[0][:, sl]) + 0.5
        c4 = (-0.5 * LRU_C) * sp[:, sl]
        log_a = c4 * tr + c4
        a = jnp.exp(log_a)
        one_m_a2 = -jnp.tanh(log_a) * (a * a + 1.0)
        root = jnp.where(one_m_a2 > 0.0, one_m_a2 * lax.rsqrt(one_m_a2), 0.0)
        a_ref[n] = a
        b_ref[n] = jnp.where(valid, root * (ig * xcn), 0.0)

    def vreg(j):
        jj = seg - 1 - j if rev else j
        return jj, pl.ds(pl.multiple_of(jj * 8, 8), 8)

    def local(j, carry):
        hs, ps = carry
        _, rows8 = vreg(j)
        nh, npr = [], []
        for n in range(LRU_BLOCKS):
            av = a_ref[n, rows8, :]
            h = av * hs[n] + b_ref[n, rows8, :]
            b_ref[n, rows8, :] = h
            nh.append(h)
            npr.append(av * ps[n])
        return tuple(nh), tuple(npr)

    zeros = tuple(jnp.zeros((8, LRU_BLOCK), F32) for _ in range(LRU_BLOCKS))
    ones = tuple(jnp.ones((8, LRU_BLOCK), F32) for _ in range(LRU_BLOCKS))
    h_end, p_end = lax.fori_loop(0, seg, local, (zeros, ones))

    cins = []
    for n in range(LRU_BLOCKS):
        c = h_ref[n]
        rows = [None] * 8
        for s in (range(7, -1, -1) if rev else range(8)):
            rows[s] = c
            c = p_end[n][s:s + 1] * c + h_end[n][s:s + 1]
        h_ref[n] = c
        cins.append(jnp.concatenate(rows, axis=0))

    def fixup(j, ps):
        jj, rows8 = vreg(j)
        strided = pl.ds(jj, 8, stride=seg)
        npr = []
        for n in range(LRU_BLOCKS):
            p = a_ref[n, rows8, :] * ps[n]
            h = b_ref[n, rows8, :] + p * cins[n]
            if rev:
                g = gates[n][0, strided, :]
                on_ref[n, strided, :] = (hf_ref[n, 0, strided, :] + h) * (g * _sigmoid(g))
            else:
                o_ref[n, 0, strided, :] = h
            npr.append(p)
        return tuple(npr)

    lax.fori_loop(0, seg, fixup, ones)
    if rev:
        for n in range(LRU_BLOCKS):
            o_ref[0, :, n * LRU_BLOCK:(n + 1) * LRU_BLOCK] = on_ref[n].astype(o_ref.dtype)


def _lru_pass(cols1, cw, cb, w, ba, bx, lam, layer, rev, tt, h_fwd=None):
    bsz, lq, _ = cols1.shape
    nt = lq // tt
    d = 1 if rev else 0
    nh = lq // HALO
    th = tt // HALO
    tmap = (lambda i: nt - 1 - i) if rev else (lambda i: i)

    def slabs(col0):
        return [pl.BlockSpec((1, tt, LRU_BLOCK), functools.partial(lambda b, i, n: (b, tmap(i), n), n=col0 + n))
                for n in range(LRU_BLOCKS)]

    in_specs = slabs(C1_X // LRU_BLOCK) + [
        pl.BlockSpec((1, HALO, LRU_WIDTH), lambda b, i: (b, jnp.maximum(tmap(i) * th - 1, 0), 0)),
        pl.BlockSpec((1, HALO, LRU_WIDTH), lambda b, i: (b, jnp.minimum((tmap(i) + 1) * th, nh - 1), 0)),
        pl.BlockSpec((1, 8, LRU_WIDTH), lambda b, i: (layer, 0, 0)),
        pl.BlockSpec((1, 1, LRU_WIDTH), lambda b, i: (layer, 0, 0)),
        pl.BlockSpec((1, LRU_BLOCKS, LRU_BLOCK, 2 * LRU_BLOCK), lambda b, i: (layer * 2 + d, 0, 0, 0)),
        pl.BlockSpec((1, 1, LRU_WIDTH), lambda b, i: (layer * 2 + d, 0, 0)),
        pl.BlockSpec((1, 1, LRU_WIDTH), lambda b, i: (layer * 2 + d, 0, 0)),
        pl.BlockSpec((1, 1, LRU_WIDTH), lambda b, i: (layer * 2 + d, 0, 0)),
    ]
    args = [cols1] * (LRU_BLOCKS + 2) + [cw, cb, w, ba, bx, lam]
    scratch = [
        pltpu.VMEM((LRU_BLOCKS, tt, LRU_BLOCK), F32),
        pltpu.VMEM((LRU_BLOCKS, tt, LRU_BLOCK), F32),
        pltpu.VMEM((LRU_BLOCKS, 1, LRU_BLOCK), F32),
    ]
    if rev:
        in_specs += [pl.BlockSpec((LRU_BLOCKS, 1, tt, LRU_BLOCK), lambda b, i: (0, b, tmap(i), 0))]
        in_specs += slabs(C1_GC // LRU_BLOCK)
        args += [h_fwd] + [cols1] * LRU_BLOCKS
        scratch += [pltpu.VMEM((LRU_BLOCKS, tt, LRU_BLOCK), F32)]
        out_spec = pl.BlockSpec((1, tt, LRU_WIDTH), lambda b, i: (b, tmap(i), 0))
        out_shape = jax.ShapeDtypeStruct((bsz, lq, LRU_WIDTH), BF16)
    else:
        out_spec = pl.BlockSpec((LRU_BLOCKS, 1, tt, LRU_BLOCK), lambda b, i: (0, b, tmap(i), 0))
        out_shape = jax.ShapeDtypeStruct((LRU_BLOCKS, bsz, lq, LRU_BLOCK), F32)
    return pl.pallas_call(
        functools.partial(_lru_kernel, rev=rev, tt=tt, nt=nt),
        grid=(bsz, nt),
        in_specs=in_specs,
        out_specs=out_spec,
        out_shape=out_shape,
        scratch_shapes=scratch,
        compiler_params=_params(("parallel", "arbitrary")),
        name="lru_bwd" if rev else "lru_fwd",
    )(*args)


def _out_kernel(z_ref, g_ref, mask_ref, ys_ref, ga_ref, yb_ref, yc_ref, wglu_ref, bglu_ref,
                wm0_ref, wm1_ref, wm2_ref, woa_ref, wob_ref, woc_ref, wo_ref, fg_ref,
                o_ref, h_ref, ya_ref, *, final):
    j = pl.program_id(1)
    nj = pl.num_programs(1)

    @pl.when(j == 0)
    def _():
        z = z_ref[...]
        ms = jnp.mean(z * z, axis=-1, keepdims=True)
        h_ref[...] = (z * lax.rsqrt(ms + EPS) * g_ref[0]).astype(BF16)
        zz = jax.nn.gelu(ys_ref[...])
        glu = _sigmoid(_dot(zz.astype(BF16), wglu_ref[0]) + bglu_ref[0])
        ga = ga_ref[...]
        ya_ref[...] = (zz * glu * (ga * _sigmoid(ga))).astype(BF16)

    h = h_ref[...]
    m = (_sigmoid(_dot(h, wm0_ref[0])) * _dot(ya_ref[...], woa_ref[0])
         + _sigmoid(_dot(h, wm1_ref[0])) * _dot(yb_ref[...], wob_ref[0])
         + _sigmoid(_dot(h, wm2_ref[0])) * _dot(yc_ref[...], woc_ref[0]))
    contrib = _dot(m.astype(BF16), wo_ref[0])

    @pl.when(j == 0)
    def _():
        o_ref[...] = contrib

    @pl.when(j > 0)
    def _():
        o_ref[...] += contrib

    @pl.when(j == nj - 1)
    def _():
        zn = z_ref[...] + o_ref[...]
        if final:
            ms = jnp.mean(zn * zn, axis=-1, keepdims=True)
            o_ref[...] = zn * lax.rsqrt(ms + EPS) * fg_ref[...]
        else:
            o_ref[...] = jnp.where(mask_ref[...] > 0.0, zn, 0.0)


def _out_proj(z, g, mask, cols1, ys, yb, yc, wglu, bglu, wm, woa, wob, woc, wo, fg, layer, tm, TJ, final):
    r, d = z.shape
    nj = d // TJ
    return pl.pallas_call(
        functools.partial(_out_kernel, final=final),
        grid=(r // tm, nj),
        in_specs=[
            pl.BlockSpec((tm, d), lambda i, j: (i, 0)),
            pl.BlockSpec((1, 1, d), lambda i, j: (layer, 0, 0)),
            pl.BlockSpec((tm, 1), lambda i, j: (i, 0)),
            pl.BlockSpec((tm, S5_WIDTH), lambda i, j: (i, 0)),
            pl.BlockSpec((tm, S5_WIDTH), lambda i, j: (i, C1_GA // S5_WIDTH)),
            pl.BlockSpec((tm, GLA_WIDTH), lambda i, j: (i, 0)),
            pl.BlockSpec((tm, LRU_WIDTH), lambda i, j: (i, 0)),
            pl.BlockSpec((1, S5_WIDTH, S5_WIDTH), lambda i, j: (layer, 0, 0)),
            pl.BlockSpec((1, 1, S5_WIDTH), lambda i, j: (layer, 0, 0)),
            pl.BlockSpec((1, d, TJ), lambda i, j: (layer, 0, j)),
            pl.BlockSpec((1, d, TJ), lambda i, j: (layer, 0, nj + j)),
            pl.BlockSpec((1, d, TJ), lambda i, j: (layer, 0, 2 * nj + j)),
            pl.BlockSpec((1, S5_WIDTH, TJ), lambda i, j: (layer, 0, j)),
            pl.BlockSpec((1, GLA_WIDTH, TJ), lambda i, j: (layer, 0, j)),
            pl.BlockSpec((1, LRU_WIDTH, TJ), lambda i, j: (layer, 0, j)),
            pl.BlockSpec((1, TJ, d), lambda i, j: (layer, j, 0)),
            pl.BlockSpec((1, d), lambda i, j: (0, 0)),
        ],
        out_specs=pl.BlockSpec((tm, d), lambda i, j: (i, 0)),
        out_shape=jax.ShapeDtypeStruct((r, d), F32),
        scratch_shapes=[
            pltpu.VMEM((tm, d), BF16),
            pltpu.VMEM((tm, S5_WIDTH), BF16),
        ],
        compiler_params=_params(("parallel", "arbitrary")),
        name="out_proj",
    )(z, g, mask, ys, cols1, yb, yc, wglu, bglu, wm, wm, wm, woa, wob, woc, wo, fg)


def _prepare(p):
    depth = p["w_in"].shape[0]
    w_in = p["w_in"]
    o = 0
    seg = {}
    for name, width in (("u", 512), ("ga", 512), ("q", 256), ("k", 256), ("v", 512), ("gb", 512),
                        ("glr", 2 * GLA_RANK), ("x", 1024), ("gc", 1024), ("m", 3 * w_in.shape[1])):
        seg[name] = w_in[:, :, o:o + width]
        o += width
    w1 = jnp.concatenate([seg["x"], seg["gc"], seg["u"], seg["ga"], seg["v"], seg["gb"]], axis=-1).astype(BF16)
    zpad = jnp.zeros(w_in.shape[:2] + (N2 - 2 * GLA_KEY - 2 * GLA_RANK,), w_in.dtype)
    w2 = jnp.concatenate([seg["q"], seg["k"], seg["glr"], zpad], axis=-1).astype(BF16)
    wm = seg["m"].astype(BF16)

    ng = S5_WIDTH // 128
    eye = jnp.eye(S5_GB, dtype=F32)

    def rows(a):
        return a.astype(F32).reshape(depth, 2, ng, S5_NS)

    ls = jnp.broadcast_to(p["s5_log_step"].astype(F32)[..., None], p["s5_lam_re"].shape)
    lr, li, ls = rows(p["s5_lam_re"]), rows(p["s5_lam_im"]), rows(ls)
    zero = jnp.zeros_like(lr[:, 0])
    prm = jnp.stack([lr[:, 0], li[:, 0], ls[:, 0], lr[:, 1], li[:, 1], ls[:, 1], zero, zero], axis=2)

    def emb_b(bm):
        bm = bm.astype(F32).reshape(depth, ng, S5_GB, 64, 16)
        return jnp.einsum("lGgnc,gh->lGgchn", bm, eye).reshape(depth, ng, 128, S5_NS)

    def emb_c(cm):
        cm = cm.astype(F32).reshape(depth, 2, ng, S5_GB, 16, 64)
        return jnp.einsum("ldGgcn,gh->ldGgchn", cm, eye).reshape(depth, 2, ng, 128, S5_NS)

    wg = jnp.zeros((depth, 2, 128, GLA_KEY), F32)
    for d in range(2):
        wg = wg.at[:, d, d * GLA_RANK:(d + 1) * GLA_RANK, :].set(p["gla_w_gate_up"][:, d].astype(F32))
    lru_w = (0.5 * jnp.concatenate([p["lru_w_a"], p["lru_w_x"]], axis=-1)).astype(BF16)
    cw = jnp.concatenate([p["conv_w"].astype(F32),
                          jnp.zeros((depth, 8 - CONV_WIDTH, LRU_WIDTH), F32)], axis=1)
    return dict(
        norm_g=p["norm_g"].astype(F32).reshape(depth, 1, -1), w1=w1, w2=w2, wm=wm,
        s5_prm=prm, s5_lbr=emb_b(p["s5_b_re"]), s5_lbi=emb_b(p["s5_b_im"]),
        s5_ctr=emb_c(p["s5_c_re"]), s5_cti=emb_c(p["s5_c_im"]),
        s5_d=p["s5_d"].astype(F32).reshape(depth, ng, 1, 128),
        wglu=p["s5_w_glu"].astype(BF16), bglu=p["s5_b_glu"].astype(F32).reshape(depth, 1, S5_WIDTH),
        wg=wg.reshape(depth * 2, 128, GLA_KEY).astype(BF16),
        bg=p["gla_b_gate"].astype(F32).reshape(depth * 2, 1, GLA_KEY),
        gla_ng=p["gla_norm_g"].astype(F32).reshape(depth, 1, GLA_WIDTH),
        cw=cw, cb=p["conv_b"].astype(F32).reshape(depth, 1, LRU_WIDTH),
        lru_w=lru_w.reshape(depth * 2, LRU_BLOCKS, LRU_BLOCK, 2 * LRU_BLOCK),
        lru_ba=p["lru_b_a"].astype(F32).reshape(depth * 2, 1, LRU_WIDTH),
        lru_bx=p["lru_b_x"].astype(F32).reshape(depth * 2, 1, LRU_WIDTH),
        lru_lam=p["lru_lam"].astype(F32).reshape(depth * 2, 1, LRU_WIDTH),
        woa=p["w_out_a"].astype(BF16), wob=p["w_out_b"].astype(BF16), woc=p["w_out_c"].astype(BF16),
        wo=p["w_o"].astype(BF16), fg=p["final_norm_g"].astype(F32).reshape(1, -1),
    )


def _pick_tile(n, unit, target):
    best = unit
    for t in range(unit, target + 1, unit):
        if n % t == 0:
            best = t
    return best


def _pick_lru_tile(lq, target):
    best = None
    for seg in range(1, target // 8 + 1):
        if lq % (8 * seg) == 0 and seg % 8 != 0:
            best = 8 * seg
    assert best is not None
    return best


def _encoder(x, meta, w):
    bsz, seq, d = x.shape
    depth = w["w2"].shape[0]
    lq = PAD + N_META + seq
    r = bsz * lq
    head = jnp.concatenate([jnp.zeros((PAD, d), F32), meta.astype(F32)], axis=0)
    z = jnp.concatenate([jnp.broadcast_to(head[None], (bsz, PAD + N_META, d)), x.astype(F32)], axis=1)
    z = z.reshape(r, d)
    mask = jnp.broadcast_to((jnp.arange(lq) >= PAD).astype(F32)[None, :, None], (bsz, lq, 1)).reshape(r, 1)
    tm = _pick_tile(r, 128, 768)
    tm_out = _pick_tile(r, 128, 640)
    tj = TJS[1] if tm_out <= 512 else TJS[0]
    cpt = _pick_tile(lq // GLA_CHUNK, 1, 8)
    tt = _pick_lru_tile(lq, 640)
    for l in range(depth):
        cols1, cols2 = _in_proj(z, w["norm_g"], w["w1"], w["w2"], l, tm)
        c1 = cols1.reshape(bsz, lq, N1)
        c2 = cols2.reshape(bsz, lq, N2)
        ys = _s5(c1, w["s5_prm"], w["s5_lbr"], w["s5_lbi"], w["s5_ctr"], w["s5_cti"], w["s5_d"], l)
        of = _gla_pass(c1, c2, w["wg"], w["bg"], l, False, cpt)
        yb = _gla_pass(c1, c2, w["wg"], w["bg"], l, True, cpt, of, w["gla_ng"])
        hf = _lru_pass(c1, w["cw"], w["cb"], w["lru_w"], w["lru_ba"], w["lru_bx"], w["lru_lam"], l, False, tt)
        yc = _lru_pass(c1, w["cw"], w["cb"], w["lru_w"], w["lru_ba"], w["lru_bx"], w["lru_lam"], l, True, tt, hf)
        z = _out_proj(z, w["norm_g"], mask, cols1, ys.reshape(r, -1), yb.reshape(r, -1),
                      yc.reshape(r, -1), w["wglu"], w["bglu"], w["wm"], w["woa"], w["wob"], w["woc"],
                      w["wo"], w["fg"], l, tm_out, tj, l == depth - 1)
    return z.reshape(bsz, lq, d)[:, PAD + N_META:].astype(x.dtype)


def kernel(x_prompt, x_sample, meta_tokens, norm_g, w_in, s5_lam_re, s5_lam_im, s5_log_step, s5_b_re, s5_b_im, s5_c_re, s5_c_im, s5_d, s5_w_glu, s5_b_glu, gla_w_gate_up, gla_b_gate, gla_norm_g, conv_w, conv_b, lru_w_a, lru_b_a, lru_w_x, lru_b_x, lru_lam, w_out_a, w_out_b, w_out_c, w_o, final_norm_g):
    w = _prepare(dict(
        norm_g=norm_g, w_in=w_in, s5_lam_re=s5_lam_re, s5_lam_im=s5_lam_im, s5_log_step=s5_log_step,
        s5_b_re=s5_b_re, s5_b_im=s5_b_im, s5_c_re=s5_c_re, s5_c_im=s5_c_im, s5_d=s5_d,
        s5_w_glu=s5_w_glu, s5_b_glu=s5_b_glu, gla_w_gate_up=gla_w_gate_up, gla_b_gate=gla_b_gate,
        gla_norm_g=gla_norm_g, conv_w=conv_w, conv_b=conv_b, lru_w_a=lru_w_a, lru_b_a=lru_b_a,
        lru_w_x=lru_w_x, lru_b_x=lru_b_x, lru_lam=lru_lam, w_out_a=w_out_a, w_out_b=w_out_b,
        w_out_c=w_out_c, w_o=w_o, final_norm_g=final_norm_g))
    return (_encoder(x_prompt, meta_tokens, w), _encoder(x_sample, meta_tokens, w))
```

```python
import functools

import jax
import jax.numpy as jnp
from jax import lax
from jax.experimental import pallas as pl
from jax.experimental.pallas import tpu as pltpu

F32 = jnp.float32
BF16 = jnp.bfloat16
HIGHEST = lax.Precision.HIGHEST

N_META = 16
PAD = 48
EPS = 1e-6
S5_WIDTH = 512
S5_T = 16
S5_GB = 8
S5_NS = 512
GLA_HEADS = 4
GLA_DK = 64
GLA_DV = 128
GLA_KEY = 256
GLA_WIDTH = 512
GLA_RANK = 16
GLA_GATE_NORM = 16.0
GLA_CHUNK = 64
LRU_WIDTH = 1024
LRU_BLOCKS = 8
LRU_BLOCK = 128
CONV_WIDTH = 4
LRU_C = 8.0
HALO = 8
VMEM_LIMIT = 56 * 1024 * 1024

C1_X, C1_GC, C1_U, C1_GA, C1_V, C1_GB = 0, 1024, 2048, 2560, 3072, 3584
N1 = 4096
N2 = 640
TN = 1024
NJ1 = N1 // TN
TJ_WIDE, TJ_NARROW = 512, 256


def _dot(a, b):
    return jnp.dot(a, b, preferred_element_type=F32)


def _dot_nt(a, b):
    return lax.dot_general(a, b, (((1,), (1,)), ((), ())), preferred_element_type=F32)


def _dot_nt_hi(a, b):
    return lax.dot_general(a, b, (((1,), (1,)), ((), ())), precision=HIGHEST, preferred_element_type=F32)


def _dot_tn(a, b):
    return lax.dot_general(a, b, (((0,), (0,)), ((), ())), preferred_element_type=F32)


def _sigmoid(x):
    return 0.5 * jnp.tanh(0.5 * x) + 0.5


def _params(sem):
    return pltpu.CompilerParams(dimension_semantics=sem, vmem_limit_bytes=VMEM_LIMIT)


def _kin_kernel(z_ref, g_ref, w1_ref, w2_ref, o1_ref, o2_ref, h_ref):
    j = pl.program_id(1)

    @pl.when(j == 0)
    def _():
        z = z_ref[...]
        ms = jnp.mean(z * z, axis=-1, keepdims=True)
        h_ref[...] = (z * lax.rsqrt(ms + EPS) * g_ref[0]).astype(BF16)

    @pl.when(j < NJ1)
    def _():
        o1_ref[...] = _dot(h_ref[...], w1_ref[0])

    @pl.when(j == NJ1)
    def _():
        o2_ref[...] = _dot(h_ref[...], w2_ref[0])


def _in_proj(z, g, w1, w2, layer, tm):
    r, d = z.shape
    last = NJ1 - 1
    return pl.pallas_call(
        _kin_kernel,
        grid=(r // tm, NJ1 + 1),
        in_specs=[
            pl.BlockSpec((tm, d), lambda i, j: (i, 0)),
            pl.BlockSpec((1, 1, d), lambda i, j: (layer, 0, 0)),
            pl.BlockSpec((1, d, TN), lambda i, j: (layer, 0, jnp.minimum(j, last))),
            pl.BlockSpec((1, d, N2), lambda i, j: (layer, 0, 0)),
        ],
        out_specs=[
            pl.BlockSpec((tm, TN), lambda i, j: (i, jnp.minimum(j, last))),
            pl.BlockSpec((tm, N2), lambda i, j: (i, 0)),
        ],
        out_shape=[jax.ShapeDtypeStruct((r, N1), F32), jax.ShapeDtypeStruct((r, N2), F32)],
        scratch_shapes=[pltpu.VMEM((tm, d), BF16)],
        compiler_params=_params(("parallel", "arbitrary")),
        name="in_proj",
    )(z, g, w1, w2)


def _s5_kernel(u_ref, prm_ref, lbr_ref, lbi_ref, ctr_ref, cti_ref, dsk_ref, o_ref,
               wst_ref, m_ref, wout_ref, e_ref, w_ref, s_ref, a16_ref, x_ref, *, nc, bsub):
    b = pl.program_id(1)
    ns = S5_NS
    blk = S5_GB * 16
    ncp = -(-nc // 8) * 8

    @pl.when(b == 0)
    def _build():
        prm = prm_ref[0, 0]
        evec = lax.broadcasted_iota(jnp.int32, (24, 1), 0).astype(F32)
        e_ref[...] = jnp.zeros_like(e_ref)
        for d in range(2):
            lr = prm[3 * d:3 * d + 1]
            li = prm[3 * d + 1:3 * d + 2]
            dt = jnp.exp(prm[3 * d + 2:3 * d + 3])
            mag = jnp.exp(evec * (lr * dt))
            ang = evec * (li * dt)
            pre = mag * jnp.cos(ang)
            pim = mag * jnp.sin(ang)
            abr = pre[1:2]
            abi = pim[1:2]
            den = lr * lr + li * li
            fr = ((abr - 1.0) * lr + abi * li) / den
            fi = (abi * lr - (abr - 1.0) * li) / den
            lbr = lbr_ref[0, 0]
            lbi = lbi_ref[0, 0]
            bbr = fr * lbr - fi * lbi
            bbi = fr * lbi + fi * lbr
            ctr = ctr_ref[0, d, 0]
            cti = cti_ref[0, d, 0]
            a16_ref[2 * d:2 * d + 1, :] = pre[S5_T:S5_T + 1]
            a16_ref[2 * d + 1:2 * d + 2, :] = pim[S5_T:S5_T + 1]
            sr, si = jnp.ones_like(lr), jnp.zeros_like(lr)
            br_, bi_ = pre[S5_T:S5_T + 1], pim[S5_T:S5_T + 1]
            nseg = ncp // 8
            while nseg:
                if nseg & 1:
                    sr, si = sr * br_ - si * bi_, sr * bi_ + si * br_
                br_, bi_ = br_ * br_ - bi_ * bi_, 2.0 * br_ * bi_
                nseg >>= 1
            a16_ref[4 + 2 * d:5 + 2 * d, :] = sr
            a16_ref[5 + 2 * d:6 + 2 * d, :] = si
            for r in range(S5_T):
                e = S5_T - 1 - r if d == 0 else r
                pr_e = pre[e:e + 1]
                pi_e = pim[e:e + 1]
                l_re = bbr * pr_e - bbi * pi_e
                l_im = bbr * pi_e + bbi * pr_e
                base = 2 * d * ns
                wst_ref[r * blk:(r + 1) * blk, base:base + ns] = l_re.astype(BF16)
                wst_ref[r * blk:(r + 1) * blk, base + ns:base + 2 * ns] = l_im.astype(BF16)
                kblk = _dot_nt(l_re.astype(BF16), ctr.astype(BF16)) - _dot_nt(l_im.astype(BF16), cti.astype(BF16))
                eb = r if d == 0 else S5_T - 1 + r
                e_ref[eb * blk:(eb + 1) * blk, :] += kblk
                eo = r + 1 if d == 0 else S5_T - r
                pr_o = pre[eo:eo + 1]
                pi_o = pim[eo:eo + 1]
                wout_ref[r * blk:(r + 1) * blk, base:base + ns] = (pr_o * ctr - pi_o * cti).astype(BF16)
                wout_ref[r * blk:(r + 1) * blk, base + ns:base + 2 * ns] = (
                    -(pi_o * ctr) - pr_o * cti).astype(BF16)
        for t in range(S5_T):
            off = (S5_T - 1 - t) * blk
            m_ref[:, t * blk:(t + 1) * blk] = e_ref[off:off + S5_T * blk, :].astype(BF16)

    for bb in range(bsub):
        for t in range(S5_T):
            x_ref[bb * ncp:bb * ncp + nc, t * blk:(t + 1) * blk] = u_ref[bb, pl.ds(t, nc, stride=S5_T), :]
        if ncp > nc:
            x_ref[bb * ncp + nc:(bb + 1) * ncp, :] = jnp.zeros((ncp - nc, S5_T * blk), F32)
    x = x_ref[...]
    xb = x.astype(BF16)
    wv = _dot(xb, wst_ref[...])
    nq = ns // 128
    for kk in range(4 * nq):
        w_ref[kk] = wv[:, kk * 128:(kk + 1) * 128]

    seg = ncp // 8
    apow = a16_ref[...]
    pairs = [(d, q) for d in range(2) for q in range(nq)]

    def coef(row, q):
        return apow[row:row + 1, q * 128:(q + 1) * 128]

    zero8 = jnp.zeros((8, 128), F32)
    one8 = jnp.ones((8, 128), F32)

    def scan_sequence(base):
        def rows(d, q, j):
            jj = base + (j if d == 0 else seg - 1 - j)
            return ((2 * d * nq + q, pl.ds(jj, 8, stride=seg), slice(None)),
                    ((2 * d + 1) * nq + q, pl.ds(jj, 8, stride=seg), slice(None)))

        def local(j, carry):
            new = []
            for n, (d, q) in enumerate(pairs):
                cr, ci = carry[2 * n], carry[2 * n + 1]
                ir, ii = rows(d, q, j)
                ar, ai = coef(2 * d, q), coef(2 * d + 1, q)
                s_ref[ir] = cr
                s_ref[ii] = ci
                new += [ar * cr - ai * ci + w_ref[ir], ar * ci + ai * cr + w_ref[ii]]
            return tuple(new)

        totals = lax.fori_loop(0, seg, local, tuple(zero8 for _ in range(2 * len(pairs))))

        cins = []
        for n, (d, q) in enumerate(pairs):
            tr, ti = totals[2 * n], totals[2 * n + 1]
            sr, si = coef(4 + 2 * d, q), coef(5 + 2 * d, q)
            cr = jnp.zeros((1, 128), F32)
            ci = jnp.zeros((1, 128), F32)
            rr, ri = [None] * 8, [None] * 8
            for s in (range(8) if d == 0 else range(7, -1, -1)):
                rr[s], ri[s] = cr, ci
                cr, ci = sr * cr - si * ci + tr[s:s + 1], sr * ci + si * cr + ti[s:s + 1]
            cins += [jnp.concatenate(rr, axis=0), jnp.concatenate(ri, axis=0)]

        def fixup(j, ps):
            new = []
            for n, (d, q) in enumerate(pairs):
                pr, pi = ps[2 * n], ps[2 * n + 1]
                ir, ii = rows(d, q, j)
                cr, ci = cins[2 * n], cins[2 * n + 1]
                s_ref[ir] = s_ref[ir] + (pr * cr - pi * ci)
                s_ref[ii] = s_ref[ii] + (pr * ci + pi * cr)
                ar, ai = coef(2 * d, q), coef(2 * d + 1, q)
                new += [ar * pr - ai * pi, ar * pi + ai * pr]
            return tuple(new)

        lax.fori_loop(0, seg, fixup, tuple(one8 if n % 2 == 0 else zero8 for n in range(2 * len(pairs))))

    for bb in range(bsub):
        scan_sequence(bb * ncp)

    s_in = jnp.concatenate([s_ref[kk] for kk in range(4 * nq)], axis=1).astype(BF16)
    y = (_dot(xb, m_ref[...]) + _dot_nt(s_in, wout_ref[...])
         + x * jnp.concatenate([dsk_ref[0, 0]] * S5_T, axis=1))
    for bb in range(bsub):
        for t in range(S5_T):
            o_ref[bb, pl.ds(t, nc, stride=S5_T), :] = y[bb * ncp:bb * ncp + nc, t * blk:(t + 1) * blk]


def _s5(cols1, prm, lbr, lbi, ctr, cti, dsk, layer):
    bsz, lq, _ = cols1.shape
    nc = lq // S5_T
    ncp = -(-nc // 8) * 8
    ng = S5_WIDTH // 128
    n = S5_T * 128
    ublk = C1_U // 128
    bsub = 2 if (bsz % 2 == 0 and 2 * ncp <= 384) else 1
    return pl.pallas_call(
        functools.partial(_s5_kernel, nc=nc, bsub=bsub),
        grid=(ng, bsz // bsub),
        in_specs=[
            pl.BlockSpec((bsub, lq, 128), lambda g, b: (b, 0, ublk + g)),
            pl.BlockSpec((1, 1, 8, S5_NS), lambda g, b: (layer, g, 0, 0)),
            pl.BlockSpec((1, 1, 128, S5_NS), lambda g, b: (layer, g, 0, 0)),
            pl.BlockSpec((1, 1, 128, S5_NS), lambda g, b: (layer, g, 0, 0)),
            pl.BlockSpec((1, 2, 1, 128, S5_NS), lambda g, b: (layer, 0, g, 0, 0)),
            pl.BlockSpec((1, 2, 1, 128, S5_NS), lambda g, b: (layer, 0, g, 0, 0)),
            pl.BlockSpec((1, 1, 1, 128), lambda g, b: (layer, g, 0, 0)),
        ],
        out_specs=pl.BlockSpec((bsub, lq, 128), lambda g, b: (b, 0, g)),
        out_shape=jax.ShapeDtypeStruct((bsz, lq, S5_WIDTH), F32),
        scratch_shapes=[
            pltpu.VMEM((n, 4 * S5_NS), BF16),
            pltpu.VMEM((n, n), BF16),
            pltpu.VMEM((n, 4 * S5_NS), BF16),
            pltpu.VMEM(((2 * S5_T - 1) * 128, 128), F32),
            pltpu.VMEM((4 * S5_NS // 128, bsub * ncp, 128), F32),
            pltpu.VMEM((4 * S5_NS // 128, bsub * ncp, 128), F32),
            pltpu.VMEM((8, S5_NS), F32),
            pltpu.VMEM((bsub * ncp, n), F32),
        ],
        compiler_params=_params(("arbitrary", "arbitrary")),
        name="s5_mixer",
    )(cols1, prm, lbr, lbi, ctr, cti, dsk)


def _gla_kernel(*refs, rev, cpt, nt):
    if rev:
        (q_ref, k_ref, v_ref, glr_ref, wg_ref, bg_ref, of_ref, gate_ref, ng_ref, o_ref, st_ref) = refs
    else:
        (q_ref, k_ref, v_ref, glr_ref, wg_ref, bg_ref, o_ref, st_ref) = refs
    i = pl.program_id(1)
    tile = nt - 1 - i if rev else i
    ch = GLA_CHUNK

    @pl.when(i == 0)
    def _():
        st_ref[...] = jnp.zeros_like(st_ref)

    tt = cpt * ch
    row = lax.broadcasted_iota(jnp.int32, (ch, ch), 0)
    col = lax.broadcasted_iota(jnp.int32, (ch, ch), 1)
    causal = (col >= row) if rev else (col <= row)
    tri = jnp.where(causal, 1.0, 0.0).astype(F32)

    glin = _dot(glr_ref[0].astype(BF16), wg_ref[0]) + bg_ref[0]
    g = jax.nn.log_sigmoid(glin) / GLA_GATE_NORM
    t_glob = tile * tt + lax.broadcasted_iota(jnp.int32, (tt, 1), 0)
    g = jnp.where(t_glob >= PAD, g, 0.0)
    bcs = [jnp.dot(tri, g[c * ch:(c + 1) * ch], precision=HIGHEST, preferred_element_type=F32)
           for c in range(cpt)]
    blasts = [bc[0:1] if rev else bc[ch - 1:ch] for bc in bcs]
    bcum = jnp.concatenate(bcs, axis=0)
    brel = jnp.concatenate([blasts[c] - bcs[c] for c in range(cpt)], axis=0)
    k = k_ref[0]
    qe = (q_ref[0] * (GLA_DK ** -0.5) * jnp.exp(bcum)).astype(BF16)
    ke = (k * jnp.exp(-bcum)).astype(BF16)
    kd = (k * jnp.exp(brel)).astype(BF16)
    v = v_ref[0].astype(BF16)

    npair = GLA_HEADS // 2
    kw, vw = 2 * GLA_DK, 2 * GLA_DV
    hi = lambda shape, axis, shift: lax.broadcasted_iota(jnp.int32, shape, axis) >> shift
    bm_k = hi((2 * ch, kw), 0, 6) == hi((2 * ch, kw), 1, 6)
    bm_v = hi((2 * ch, vw), 0, 6) == hi((2 * ch, vw), 1, 7)
    bm_s = hi((vw, kw), 0, 7) == hi((vw, kw), 1, 6)
    arow = lax.broadcasted_iota(jnp.int32, (ch, 2 * ch), 0)
    acol = lax.broadcasted_iota(jnp.int32, (ch, 2 * ch), 1) & (ch - 1)
    causal2 = (acol >= arow) if rev else (acol <= arow)
    zk = jnp.zeros((), BF16)
    states = [st_ref[p] for p in range(npair)]
    for ci in range(cpt):
        c = cpt - 1 - ci if rev else ci
        rs = slice(c * ch, (c + 1) * ch)
        dec = jnp.exp(blasts[c])
        outs = []
        for p in range(npair):
            ks = slice(p * kw, (p + 1) * kw)
            vs = slice(p * vw, (p + 1) * vw)
            qp = qe[rs, ks]
            ke2 = jnp.where(bm_k, jnp.concatenate([ke[rs, ks]] * 2, axis=0), zk)
            v2 = jnp.where(bm_v, jnp.concatenate([v[rs, vs]] * 2, axis=0), zk)
            att = jnp.where(causal2, _dot_nt(qp, ke2), 0.0)
            outs.append(_dot(att.astype(BF16), v2) + _dot_nt(qp, states[p].astype(BF16)))
            states[p] = states[p] * dec[:, ks] + jnp.where(bm_s, _dot_tn(v[rs, vs], kd[rs, ks]), 0.0)
        o = jnp.concatenate(outs, axis=1)
        if rev:
            o = o + of_ref[0, rs, :]
            parts = []
            for h in range(GLA_HEADS):
                oh = o[:, h * GLA_DV:(h + 1) * GLA_DV]
                parts.append(oh * lax.rsqrt(jnp.mean(oh * oh, axis=-1, keepdims=True) + EPS))
            o = jnp.concatenate(parts, axis=1) * ng_ref[0]
            gate = gate_ref[0, rs, :]
            o = o * (gate * _sigmoid(gate))
        o_ref[0, rs, :] = o.astype(o_ref.dtype)
    for p in range(npair):
        st_ref[p] = states[p]


def _gla_pass(cols1, cols2, wg, bg, layer, rev, cpt, o_fwd=None, norm_g=None):
    bsz, lq, _ = cols1.shape
    tt = cpt * GLA_CHUNK
    nt = lq // tt
    d = 1 if rev else 0
    tmap = (lambda i: nt - 1 - i) if rev else (lambda i: i)
    in_specs = [
        pl.BlockSpec((1, tt, GLA_KEY), lambda b, i: (b, tmap(i), 0)),
        pl.BlockSpec((1, tt, GLA_KEY), lambda b, i: (b, tmap(i), 1)),
        pl.BlockSpec((1, tt, GLA_WIDTH), lambda b, i: (b, tmap(i), C1_V // GLA_WIDTH)),
        pl.BlockSpec((1, tt, 128), lambda b, i: (b, tmap(i), 2 * GLA_KEY // 128)),
        pl.BlockSpec((1, 128, GLA_KEY), lambda b, i: (layer * 2 + d, 0, 0)),
        pl.BlockSpec((1, 1, GLA_KEY), lambda b, i: (layer * 2 + d, 0, 0)),
    ]
    args = [cols2, cols2, cols1, cols2, wg, bg]
    if rev:
        in_specs += [
            pl.BlockSpec((1, tt, GLA_WIDTH), lambda b, i: (b, tmap(i), 0)),
            pl.BlockSpec((1, tt, GLA_WIDTH), lambda b, i: (b, tmap(i), C1_GB // GLA_WIDTH)),
            pl.BlockSpec((1, 1, GLA_WIDTH), lambda b, i: (layer, 0, 0)),
        ]
        args += [o_fwd, cols1, norm_g]
    return pl.pallas_call(
        functools.partial(_gla_kernel, rev=rev, cpt=cpt, nt=nt),
        grid=(bsz, nt),
        in_specs=in_specs,
        out_specs=pl.BlockSpec((1, tt, GLA_WIDTH), lambda b, i: (b, tmap(i), 0)),
        out_shape=jax.ShapeDtypeStruct((bsz, lq, GLA_WIDTH), BF16 if rev else F32),
        scratch_shapes=[pltpu.VMEM((GLA_HEADS // 2, 2 * GLA_DV, 2 * GLA_DK), F32)],
        compiler_params=_params(("parallel", "arbitrary")),
        name="gla_bwd" if rev else "gla_fwd",
    )(*args)


def _lru_kernel(*refs, rev, tt, nt):
    xs = refs[:LRU_BLOCKS]
    refs = refs[LRU_BLOCKS:]
    if rev:
        (xp_ref, xn_ref, cw_ref, cb_ref, w_ref, ba_ref, bx_ref, lam_ref, hf_ref) = refs[:9]
        gates = refs[9:9 + LRU_BLOCKS]
        o_ref, a_ref, b_ref, h_ref, on_ref = refs[9 + LRU_BLOCKS:]
    else:
        (xp_ref, xn_ref, cw_ref, cb_ref, w_ref, ba_ref, bx_ref, lam_ref,
         o_ref, a_ref, b_ref, h_ref) = refs
    i = pl.program_id(1)
    tile = nt - 1 - i if rev else i
    seg = tt // 8

    @pl.when(i == 0)
    def _():
        h_ref[...] = jnp.zeros_like(h_ref)

    def permuted(ref, lead):
        return jnp.concatenate([ref[lead + (pl.ds(j, 8, stride=seg), slice(None))] for j in range(seg)], axis=0)

    prev = jnp.where(tile == 0, 0.0, xp_ref[0])
    nxt = jnp.where(tile == nt - 1, 0.0, xn_ref[0])
    sub = lax.broadcasted_iota(jnp.int32, (8, LRU_BLOCK), 0)
    left = CONV_WIDTH // 2
    cw = cw_ref[0]
    sp = jax.nn.softplus(-lam_ref[0])
    rperm = lax.broadcasted_iota(jnp.int32, (tt, 1), 0)
    valid = (tile * tt + (rperm & 7) * seg + (rperm >> 3)) >= PAD
    for n in range(LRU_BLOCKS):
        sl = slice(n * LRU_BLOCK, (n + 1) * LRU_BLOCK)
        xp = permuted(xs[n], (0,))
        before = []
        for k in range(left, 0, -1):
            edge = pltpu.roll(xp[(seg - k) * 8:(seg - k + 1) * 8], 1, axis=0)
            before.append(jnp.where(sub == 0, prev[HALO - k:HALO - k + 1, sl], edge))
        after = []
        for k in range(CONV_WIDTH - 1 - left):
            edge = pltpu.roll(xp[k * 8:(k + 1) * 8], 7, axis=0)
            after.append(jnp.where(sub == 7, nxt[k:k + 1, sl], edge))
        xext = jnp.concatenate(before + [xp] + after, axis=0)
        xcn = cb_ref[0][:, sl]
        for j in range(CONV_WIDTH):
            xcn = xcn + xext[8 * j:8 * j + tt] * cw[j:j + 1, sl]
        pre = _dot(xcn.astype(BF16), w_ref[0, n])
        tr = jnp.tanh(pre[:, :LRU_BLOCK] + 0.5 * ba_ref[0][:, sl])
        ig = 0.5 * jnp.tanh(pre[:, LRU_BLOCK:] + 0.5 * bx_ref[0][:, sl]) + 0.5
        c4 = (-0.5 * LRU_C) * sp[:, sl]
        log_a = c4 * tr + c4
        a = jnp.exp(log_a)
        one_m_a2 = -jnp.tanh(log_a) * (a * a + 1.0)
        root = jnp.where(one_m_a2 > 0.0, one_m_a2 * lax.rsqrt(one_m_a2), 0.0)
        a_ref[n] = a
        b_ref[n] = jnp.where(valid, root * (ig * xcn), 0.0)

    def vreg(j):
        jj = seg - 1 - j if rev else j
        return jj, pl.ds(pl.multiple_of(jj * 8, 8), 8)

    def local(j, carry):
        hs, ps = carry
        _, rows8 = vreg(j)
        nh, npr = [], []
        for n in range(LRU_BLOCKS):
            av = a_ref[n, rows8, :]
            h = av * hs[n] + b_ref[n, rows8, :]
            b_ref[n, rows8, :] = h
            nh.append(h)
            npr.append(av * ps[n])
        return tuple(nh), tuple(npr)

    zeros = tuple(jnp.zeros((8, LRU_BLOCK), F32) for _ in range(LRU_BLOCKS))
    ones = tuple(jnp.ones((8, LRU_BLOCK), F32) for _ in range(LRU_BLOCKS))
    h_end, p_end = lax.fori_loop(0, seg, local, (zeros, ones))

    cins = []
    for n in range(LRU_BLOCKS):
        c = h_ref[n]
        rows = [None] * 8
        for s in (range(7, -1, -1) if rev else range(8)):
            rows[s] = c
            c = p_end[n][s:s + 1] * c + h_end[n][s:s + 1]
        h_ref[n] = c
        cins.append(jnp.concatenate(rows, axis=0))

    def fixup(j, ps):
        jj, rows8 = vreg(j)
        strided = pl.ds(jj, 8, stride=seg)
        npr = []
        for n in range(LRU_BLOCKS):
            p = a_ref[n, rows8, :] * ps[n]
            h = b_ref[n, rows8, :] + p * cins[n]
            if rev:
                g = gates[n][0, strided, :]
                on_ref[n, strided, :] = (hf_ref[n, 0, strided, :] + h) * (g * _sigmoid(g))
            else:
                o_ref[n, 0, strided, :] = h
            npr.append(p)
        return tuple(npr)

    lax.fori_loop(0, seg, fixup, ones)
    if rev:
        for n in range(LRU_BLOCKS):
            o_ref[0, :, n * LRU_BLOCK:(n + 1) * LRU_BLOCK] = on_ref[n].astype(o_ref.dtype)


def _lru_pass(cols1, cw, cb, w, ba, bx, lam, layer, rev, tt, h_fwd=None):
    bsz, lq, _ = cols1.shape
    nt = lq // tt
    d = 1 if rev else 0
    nh = lq // HALO
    th = tt // HALO
    tmap = (lambda i: nt - 1 - i) if rev else (lambda i: i)

    def slabs(col0):
        return [pl.BlockSpec((1, tt, LRU_BLOCK), functools.partial(lambda b, i, n: (b, tmap(i), n), n=col0 + n))
                for n in range(LRU_BLOCKS)]

    in_specs = slabs(C1_X // LRU_BLOCK) + [
        pl.BlockSpec((1, HALO, LRU_WIDTH), lambda b, i: (b, jnp.maximum(tmap(i) * th - 1, 0), 0)),
        pl.BlockSpec((1, HALO, LRU_WIDTH), lambda b, i: (b, jnp.minimum((tmap(i) + 1) * th, nh - 1), 0)),
        pl.BlockSpec((1, 8, LRU_WIDTH), lambda b, i: (layer, 0, 0)),
        pl.BlockSpec((1, 1, LRU_WIDTH), lambda b, i: (layer, 0, 0)),
        pl.BlockSpec((1, LRU_BLOCKS, LRU_BLOCK, 2 * LRU_BLOCK), lambda b, i: (layer * 2 + d, 0, 0, 0)),
        pl.BlockSpec((1, 1, LRU_WIDTH), lambda b, i: (layer * 2 + d, 0, 0)),
        pl.BlockSpec((1, 1, LRU_WIDTH), lambda b, i: (layer * 2 + d, 0, 0)),
        pl.BlockSpec((1, 1, LRU_WIDTH), lambda b, i: (layer * 2 + d, 0, 0)),
    ]
    args = [cols1] * (LRU_BLOCKS + 2) + [cw, cb, w, ba, bx, lam]
    scratch = [
        pltpu.VMEM((LRU_BLOCKS, tt, LRU_BLOCK), F32),
        pltpu.VMEM((LRU_BLOCKS, tt, LRU_BLOCK), F32),
        pltpu.VMEM((LRU_BLOCKS, 1, LRU_BLOCK), F32),
    ]
    if rev:
        in_specs += [pl.BlockSpec((LRU_BLOCKS, 1, tt, LRU_BLOCK), lambda b, i: (0, b, tmap(i), 0))]
        in_specs += slabs(C1_GC // LRU_BLOCK)
        args += [h_fwd] + [cols1] * LRU_BLOCKS
        scratch += [pltpu.VMEM((LRU_BLOCKS, tt, LRU_BLOCK), F32)]
        out_spec = pl.BlockSpec((1, tt, LRU_WIDTH), lambda b, i: (b, tmap(i), 0))
        out_shape = jax.ShapeDtypeStruct((bsz, lq, LRU_WIDTH), BF16)
    else:
        out_spec = pl.BlockSpec((LRU_BLOCKS, 1, tt, LRU_BLOCK), lambda b, i: (0, b, tmap(i), 0))
        out_shape = jax.ShapeDtypeStruct((LRU_BLOCKS, bsz, lq, LRU_BLOCK), F32)
    return pl.pallas_call(
        functools.partial(_lru_kernel, rev=rev, tt=tt, nt=nt),
        grid=(bsz, nt),
        in_specs=in_specs,
        out_specs=out_spec,
        out_shape=out_shape,
        scratch_shapes=scratch,
        compiler_params=_params(("parallel", "arbitrary")),
        name="lru_bwd" if rev else "lru_fwd",
    )(*args)


def _out_kernel(z_ref, g_ref, mask_ref, ys_ref, ga_ref, yb_ref, yc_ref, wglu_ref, bglu_ref,
                wm0_ref, wm1_ref, wm2_ref, woa_ref, wob_ref, woc_ref, wo_ref, fg_ref,
                o_ref, h_ref, ya_ref, *, final):
    j = pl.program_id(1)
    nj = pl.num_programs(1)

    @pl.when(j == 0)
    def _():
        z = z_ref[...]
        ms = jnp.mean(z * z, axis=-1, keepdims=True)
        h_ref[...] = (z * lax.rsqrt(ms + EPS) * g_ref[0]).astype(BF16)
        zz = jax.nn.gelu(ys_ref[...])
        glu = _sigmoid(_dot(zz.astype(BF16), wglu_ref[0]) + bglu_ref[0])
        ga = ga_ref[...]
        ya_ref[...] = (zz * glu * (ga * _sigmoid(ga))).astype(BF16)

    h = h_ref[...]
    m = (_sigmoid(_dot(h, wm0_ref[0])) * _dot(ya_ref[...], woa_ref[0])
         + _sigmoid(_dot(h, wm1_ref[0])) * _dot(yb_ref[...], wob_ref[0])
         + _sigmoid(_dot(h, wm2_ref[0])) * _dot(yc_ref[...], woc_ref[0]))
    contrib = _dot(m.astype(BF16), wo_ref[0])

    @pl.when(j == 0)
    def _():
        o_ref[...] = contrib

    @pl.when(j > 0)
    def _():
        o_ref[...] += contrib

    @pl.when(j == nj - 1)
    def _():
        zn = z_ref[...] + o_ref[...]
        if final:
            ms = jnp.mean(zn * zn, axis=-1, keepdims=True)
            o_ref[...] = zn * lax.rsqrt(ms + EPS) * fg_ref[...]
        else:
            o_ref[...] = jnp.where(mask_ref[...] > 0.0, zn, 0.0)


def _out_proj(z, g, mask, cols1, ys, yb, yc, wglu, bglu, wm, woa, wob, woc, wo, fg, layer, tm, final):
    r, d = z.shape
    TJ = TJ_WIDE if tm <= 512 else TJ_NARROW
    nj = d // TJ
    return pl.pallas_call(
        functools.partial(_out_kernel, final=final),
        grid=(r // tm, nj),
        in_specs=[
            pl.BlockSpec((tm, d), lambda i, j: (i, 0)),
            pl.BlockSpec((1, 1, d), lambda i, j: (layer, 0, 0)),
            pl.BlockSpec((tm, 1), lambda i, j: (i, 0)),
            pl.BlockSpec((tm, S5_WIDTH), lambda i, j: (i, 0)),
            pl.BlockSpec((tm, S5_WIDTH), lambda i, j: (i, C1_GA // S5_WIDTH)),
            pl.BlockSpec((tm, GLA_WIDTH), lambda i, j: (i, 0)),
            pl.BlockSpec((tm, LRU_WIDTH), lambda i, j: (i, 0)),
            pl.BlockSpec((1, S5_WIDTH, S5_WIDTH), lambda i, j: (layer, 0, 0)),
            pl.BlockSpec((1, 1, S5_WIDTH), lambda i, j: (layer, 0, 0)),
            pl.BlockSpec((1, d, TJ), lambda i, j: (layer, 0, j)),
            pl.BlockSpec((1, d, TJ), lambda i, j: (layer, 0, nj + j)),
            pl.BlockSpec((1, d, TJ), lambda i, j: (layer, 0, 2 * nj + j)),
            pl.BlockSpec((1, S5_WIDTH, TJ), lambda i, j: (layer, 0, j)),
            pl.BlockSpec((1, GLA_WIDTH, TJ), lambda i, j: (layer, 0, j)),
            pl.BlockSpec((1, LRU_WIDTH, TJ), lambda i, j: (layer, 0, j)),
            pl.BlockSpec((1, TJ, d), lambda i, j: (layer, j, 0)),
            pl.BlockSpec((1, d), lambda i, j: (0, 0)),
        ],
        out_specs=pl.BlockSpec((tm, d), lambda i, j: (i, 0)),
        out_shape=jax.ShapeDtypeStruct((r, d), F32),
        scratch_shapes=[
            pltpu.VMEM((tm, d), BF16),
            pltpu.VMEM((tm, S5_WIDTH), BF16),
        ],
        compiler_params=_params(("parallel", "arbitrary")),
        name="out_proj",
    )(z, g, mask, ys, cols1, yb, yc, wglu, bglu, wm, wm, wm, woa, wob, woc, wo, fg)


def _prepare(p):
    depth = p["w_in"].shape[0]
    w_in = p["w_in"]
    o = 0
    seg = {}
    for name, width in (("u", 512), ("ga", 512), ("q", 256), ("k", 256), ("v", 512), ("gb", 512),
                        ("glr", 2 * GLA_RANK), ("x", 1024), ("gc", 1024), ("m", 3 * w_in.shape[1])):
        seg[name] = w_in[:, :, o:o + width]
        o += width
    w1 = jnp.concatenate([seg["x"], seg["gc"], seg["u"], seg["ga"], seg["v"], seg["gb"]], axis=-1).astype(BF16)
    zpad = jnp.zeros(w_in.shape[:2] + (N2 - 2 * GLA_KEY - 2 * GLA_RANK,), w_in.dtype)
    w2 = jnp.concatenate([seg["q"], seg["k"], seg["glr"], zpad], axis=-1).astype(BF16)
    wm = seg["m"].astype(BF16)

    ng = S5_WIDTH // 128
    eye = jnp.eye(S5_GB, dtype=F32)

    def rows(a):
        return a.astype(F32).reshape(depth, 2, ng, S5_NS)

    ls = jnp.broadcast_to(p["s5_log_step"].astype(F32)[..., None], p["s5_lam_re"].shape)
    lr, li, ls = rows(p["s5_lam_re"]), rows(p["s5_lam_im"]), rows(ls)
    zero = jnp.zeros_like(lr[:, 0])
    prm = jnp.stack([lr[:, 0], li[:, 0], ls[:, 0], lr[:, 1], li[:, 1], ls[:, 1], zero, zero], axis=2)

    def emb_b(bm):
        bm = bm.astype(F32).reshape(depth, ng, S5_GB, 64, 16)
        return jnp.einsum("lGgnc,gh->lGgchn", bm, eye).reshape(depth, ng, 128, S5_NS)

    def emb_c(cm):
        cm = cm.astype(F32).reshape(depth, 2, ng, S5_GB, 16, 64)
        return jnp.einsum("ldGgcn,gh->ldGgchn", cm, eye).reshape(depth, 2, ng, 128, S5_NS)

    wg = jnp.zeros((depth, 2, 128, GLA_KEY), F32)
    for d in range(2):
        wg = wg.at[:, d, d * GLA_RANK:(d + 1) * GLA_RANK, :].set(p["gla_w_gate_up"][:, d].astype(F32))
    lru_w = (0.5 * jnp.concatenate([p["lru_w_a"], p["lru_w_x"]], axis=-1)).astype(BF16)
    cw = jnp.concatenate([p["conv_w"].astype(F32),
                          jnp.zeros((depth, 8 - CONV_WIDTH, LRU_WIDTH), F32)], axis=1)
    return dict(
        norm_g=p["norm_g"].astype(F32).reshape(depth, 1, -1), w1=w1, w2=w2, wm=wm,
        s5_prm=prm, s5_lbr=emb_b(p["s5_b_re"]), s5_lbi=emb_b(p["s5_b_im"]),
        s5_ctr=emb_c(p["s5_c_re"]), s5_cti=emb_c(p["s5_c_im"]),
        s5_d=p["s5_d"].astype(F32).reshape(depth, ng, 1, 128),
        wglu=p["s5_w_glu"].astype(BF16), bglu=p["s5_b_glu"].astype(F32).reshape(depth, 1, S5_WIDTH),
        wg=wg.reshape(depth * 2, 128, GLA_KEY).astype(BF16),
        bg=p["gla_b_gate"].astype(F32).reshape(depth * 2, 1, GLA_KEY),
        gla_ng=p["gla_norm_g"].astype(F32).reshape(depth, 1, GLA_WIDTH),
        cw=cw, cb=p["conv_b"].astype(F32).reshape(depth, 1, LRU_WIDTH),
        lru_w=lru_w.reshape(depth * 2, LRU_BLOCKS, LRU_BLOCK, 2 * LRU_BLOCK),
        lru_ba=p["lru_b_a"].astype(F32).reshape(depth * 2, 1, LRU_WIDTH),
        lru_bx=p["lru_b_x"].astype(F32).reshape(depth * 2, 1, LRU_WIDTH),
        lru_lam=p["lru_lam"].astype(F32).reshape(depth * 2, 1, LRU_WIDTH),
        woa=p["w_out_a"].astype(BF16), wob=p["w_out_b"].astype(BF16), woc=p["w_out_c"].astype(BF16),
        wo=p["w_o"].astype(BF16), fg=p["final_norm_g"].astype(F32).reshape(1, -1),
    )


def _pick_tile(n, unit, target):
    best = unit
    for t in range(unit, target + 1, unit):
        if n % t == 0:
            best = t
    return best


def _pick_lru_tile(lq, target):
    best = None
    for seg in range(1, target // 8 + 1):
        if lq % (8 * seg) == 0 and seg % 8 != 0:
            best = 8 * seg
    assert best is not None
    return best


def _encoder(x, meta, w):
    bsz, seq, d = x.shape
    depth = w["w2"].shape[0]
    lq = PAD + N_META + seq
    r = bsz * lq
    head = jnp.concatenate([jnp.zeros((PAD, d), F32), meta.astype(F32)], axis=0)
    z = jnp.concatenate([jnp.broadcast_to(head[None], (bsz, PAD + N_META, d)), x.astype(F32)], axis=1)
    z = z.reshape(r, d)
    mask = jnp.broadcast_to((jnp.arange(lq) >= PAD).astype(F32)[None, :, None], (bsz, lq, 1)).reshape(r, 1)
    tm = _pick_tile(r, 128, 768)
    tm_out = _pick_tile(r, 128, 640)
    cpt = _pick_tile(lq // GLA_CHUNK, 1, 13)
    tt = _pick_lru_tile(lq, 640)
    for l in range(depth):
        cols1, cols2 = _in_proj(z, w["norm_g"], w["w1"], w["w2"], l, tm)
        c1 = cols1.reshape(bsz, lq, N1)
        c2 = cols2.reshape(bsz, lq, N2)
        ys = _s5(c1, w["s5_prm"], w["s5_lbr"], w["s5_lbi"], w["s5_ctr"], w["s5_cti"], w["s5_d"], l)
        of = _gla_pass(c1, c2, w["wg"], w["bg"], l, False, cpt)
        yb = _gla_pass(c1, c2, w["wg"], w["bg"], l, True, cpt, of, w["gla_ng"])
        hf = _lru_pass(c1, w["cw"], w["cb"], w["lru_w"], w["lru_ba"], w["lru_bx"], w["lru_lam"], l, False, tt)
        yc = _lru_pass(c1, w["cw"], w["cb"], w["lru_w"], w["lru_ba"], w["lru_bx"], w["lru_lam"], l, True, tt, hf)
        z = _out_proj(z, w["norm_g"], mask, cols1, ys.reshape(r, -1), yb.reshape(r, -1),
                      yc.reshape(r, -1), w["wglu"], w["bglu"], w["wm"], w["woa"], w["wob"], w["woc"],
                      w["wo"], w["fg"], l, tm_out, l == depth - 1)
    return z.reshape(bsz, lq, d)[:, PAD + N_META:].astype(x.dtype)


def kernel(x_prompt, x_sample, meta_tokens, norm_g, w_in, s5_lam_re, s5_lam_im, s5_log_step, s5_b_re, s5_b_im, s5_c_re, s5_c_im, s5_d, s5_w_glu, s5_b_glu, gla_w_gate_up, gla_b_gate, gla_norm_g, conv_w, conv_b, lru_w_a, lru_b_a, lru_w_x, lru_b_x, lru_lam, w_out_a, w_out_b, w_out_c, w_o, final_norm_g):
    w = _prepare(dict(
        norm_g=norm_g, w_in=w_in, s5_lam_re=s5_lam_re, s5_lam_im=s5_lam_im, s5_log_step=s5_log_step,
        s5_b_re=s5_b_re, s5_b_im=s5_b_im, s5_c_re=s5_c_re, s5_c_im=s5_c_im, s5_d=s5_d,
        s5_w_glu=s5_w_glu, s5_b_glu=s5_b_glu, gla_w_gate_up=gla_w_gate_up, gla_b_gate=gla_b_gate,
        gla_norm_g=gla_norm_g, conv_w=conv_w, conv_b=conv_b, lru_w_a=lru_w_a, lru_b_a=lru_b_a,
        lru_w_x=lru_w_x, lru_b_x=lru_b_x, lru_lam=lru_lam, w_out_a=w_out_a, w_out_b=w_out_b,
        w_out_c=w_out_c, w_o=w_o, final_norm_g=final_norm_g))
    return (_encoder(x_prompt, meta_tokens, w), _encoder(x_sample, meta_tokens, w))
```

```python
import functools

import jax
import jax.numpy as jnp
from jax import lax
from jax.experimental import pallas as pl
from jax.experimental.pallas import tpu as pltpu

F32 = jnp.float32
BF16 = jnp.bfloat16
HIGHEST = lax.Precision.HIGHEST

N_META = 16
PAD = 48
EPS = 1e-6
S5_WIDTH = 512
S5_T = 16
S5_GB = 8
S5_NS = 512
GLA_HEADS = 4
GLA_DK = 64
GLA_DV = 128
GLA_KEY = 256
GLA_WIDTH = 512
GLA_RANK = 16
GLA_GATE_NORM = 16.0
GLA_CHUNK = 64
LRU_WIDTH = 1024
LRU_BLOCKS = 8
LRU_BLOCK = 128
CONV_WIDTH = 4
LRU_C = 8.0
HALO = 8
VMEM_LIMIT = 56 * 1024 * 1024

C1_X, C1_GC, C1_U, C1_GA = 0, 1024, 2048, 2560
N1 = 3072
C2_V, C2_GB, C2_Q, C2_K, C2_GLR = 0, 512, 1024, 1280, 1536
N2 = 1664
TN = 1024
NJ1 = N1 // TN
TJ_WIDE, TJ_NARROW = 512, 256


def _dot(a, b):
    return jnp.dot(a, b, preferred_element_type=F32)


def _dot_nt(a, b):
    return lax.dot_general(a, b, (((1,), (1,)), ((), ())), preferred_element_type=F32)


def _dot_nt_hi(a, b):
    return lax.dot_general(a, b, (((1,), (1,)), ((), ())), precision=HIGHEST, preferred_element_type=F32)


def _dot_tn(a, b):
    return lax.dot_general(a, b, (((0,), (0,)), ((), ())), preferred_element_type=F32)


def _sigmoid(x):
    return 0.5 * jnp.tanh(0.5 * x) + 0.5


def _params(sem):
    return pltpu.CompilerParams(dimension_semantics=sem, vmem_limit_bytes=VMEM_LIMIT)


def _kin_kernel(z_ref, g_ref, w1_ref, w2_ref, o1_ref, o2_ref, h_ref):
    j = pl.program_id(1)

    @pl.when(j == 0)
    def _():
        z = z_ref[...]
        ms = jnp.mean(z * z, axis=-1, keepdims=True)
        h_ref[...] = (z * lax.rsqrt(ms + EPS) * g_ref[0]).astype(BF16)

    @pl.when(j < NJ1)
    def _():
        o1_ref[...] = _dot(h_ref[...], w1_ref[0])

    @pl.when(j == NJ1)
    def _():
        o2_ref[...] = _dot(h_ref[...], w2_ref[0]).astype(o2_ref.dtype)


def _in_proj(z, g, w1, w2, layer, tm):
    r, d = z.shape
    last = NJ1 - 1
    return pl.pallas_call(
        _kin_kernel,
        grid=(r // tm, NJ1 + 1),
        in_specs=[
            pl.BlockSpec((tm, d), lambda i, j: (i, 0)),
            pl.BlockSpec((1, 1, d), lambda i, j: (layer, 0, 0)),
            pl.BlockSpec((1, d, TN), lambda i, j: (layer, 0, jnp.minimum(j, last))),
            pl.BlockSpec((1, d, N2), lambda i, j: (layer, 0, 0)),
        ],
        out_specs=[
            pl.BlockSpec((tm, TN), lambda i, j: (i, jnp.minimum(j, last))),
            pl.BlockSpec((tm, N2), lambda i, j: (i, 0)),
        ],
        out_shape=[jax.ShapeDtypeStruct((r, N1), F32), jax.ShapeDtypeStruct((r, N2), BF16)],
        scratch_shapes=[pltpu.VMEM((tm, d), BF16)],
        compiler_params=_params(("parallel", "arbitrary")),
        name="in_proj",
    )(z, g, w1, w2)


def _s5_kernel(u_ref, prm_ref, lbr_ref, lbi_ref, ctr_ref, cti_ref, dsk_ref, o_ref,
               wst_ref, m_ref, wout_ref, e_ref, w_ref, s_ref, a16_ref, x_ref, *, nc, bsub):
    b = pl.program_id(1)
    ns = S5_NS
    blk = S5_GB * 16
    ncp = -(-nc // 8) * 8

    @pl.when(b == 0)
    def _build():
        prm = prm_ref[0, 0]
        evec = lax.broadcasted_iota(jnp.int32, (24, 1), 0).astype(F32)
        e_ref[...] = jnp.zeros_like(e_ref)
        for d in range(2):
            lr = prm[3 * d:3 * d + 1]
            li = prm[3 * d + 1:3 * d + 2]
            dt = jnp.exp(prm[3 * d + 2:3 * d + 3])
            mag = jnp.exp(evec * (lr * dt))
            ang = evec * (li * dt)
            pre = mag * jnp.cos(ang)
            pim = mag * jnp.sin(ang)
            abr = pre[1:2]
            abi = pim[1:2]
            den = lr * lr + li * li
            fr = ((abr - 1.0) * lr + abi * li) / den
            fi = (abi * lr - (abr - 1.0) * li) / den
            lbr = lbr_ref[0, 0]
            lbi = lbi_ref[0, 0]
            bbr = fr * lbr - fi * lbi
            bbi = fr * lbi + fi * lbr
            ctr = ctr_ref[0, d, 0]
            cti = cti_ref[0, d, 0]
            a16_ref[2 * d:2 * d + 1, :] = pre[S5_T:S5_T + 1]
            a16_ref[2 * d + 1:2 * d + 2, :] = pim[S5_T:S5_T + 1]
            sr, si = jnp.ones_like(lr), jnp.zeros_like(lr)
            br_, bi_ = pre[S5_T:S5_T + 1], pim[S5_T:S5_T + 1]
            nseg = ncp // 8
            while nseg:
                if nseg & 1:
                    sr, si = sr * br_ - si * bi_, sr * bi_ + si * br_
                br_, bi_ = br_ * br_ - bi_ * bi_, 2.0 * br_ * bi_
                nseg >>= 1
            a16_ref[4 + 2 * d:5 + 2 * d, :] = sr
            a16_ref[5 + 2 * d:6 + 2 * d, :] = si
            for r in range(S5_T):
                e = S5_T - 1 - r if d == 0 else r
                pr_e = pre[e:e + 1]
                pi_e = pim[e:e + 1]
                l_re = bbr * pr_e - bbi * pi_e
                l_im = bbr * pi_e + bbi * pr_e
                base = 2 * d * ns
                wst_ref[r * blk:(r + 1) * blk, base:base + ns] = l_re.astype(BF16)
                wst_ref[r * blk:(r + 1) * blk, base + ns:base + 2 * ns] = l_im.astype(BF16)
                kblk = _dot_nt(l_re.astype(BF16), ctr.astype(BF16)) - _dot_nt(l_im.astype(BF16), cti.astype(BF16))
                eb = r if d == 0 else S5_T - 1 + r
                e_ref[eb * blk:(eb + 1) * blk, :] += kblk
                eo = r + 1 if d == 0 else S5_T - r
                pr_o = pre[eo:eo + 1]
                pi_o = pim[eo:eo + 1]
                wout_ref[r * blk:(r + 1) * blk, base:base + ns] = (pr_o * ctr - pi_o * cti).astype(BF16)
                wout_ref[r * blk:(r + 1) * blk, base + ns:base + 2 * ns] = (
                    -(pi_o * ctr) - pr_o * cti).astype(BF16)
        for t in range(S5_T):
            off = (S5_T - 1 - t) * blk
            m_ref[:, t * blk:(t + 1) * blk] = e_ref[off:off + S5_T * blk, :].astype(BF16)

    for bb in range(bsub):
        for t in range(S5_T):
            x_ref[bb * ncp:bb * ncp + nc, t * blk:(t + 1) * blk] = u_ref[bb, pl.ds(t, nc, stride=S5_T), :]
        if ncp > nc:
            x_ref[bb * ncp + nc:(bb + 1) * ncp, :] = jnp.zeros((ncp - nc, S5_T * blk), F32)
    x = x_ref[...]
    xb = x.astype(BF16)
    wv = _dot(xb, wst_ref[...])
    nq = ns // 128
    for kk in range(4 * nq):
        w_ref[kk] = wv[:, kk * 128:(kk + 1) * 128]

    seg = ncp // 8
    apow = a16_ref[...]
    pairs = [(d, q) for d in range(2) for q in range(nq)]

    def coef(row, q):
        return apow[row:row + 1, q * 128:(q + 1) * 128]

    zero8 = jnp.zeros((8, 128), F32)
    one8 = jnp.ones((8, 128), F32)

    def scan_sequence(base):
        def rows(d, q, j):
            jj = base + (j if d == 0 else seg - 1 - j)
            return ((2 * d * nq + q, pl.ds(jj, 8, stride=seg), slice(None)),
                    ((2 * d + 1) * nq + q, pl.ds(jj, 8, stride=seg), slice(None)))

        def local(j, carry):
            new = []
            for n, (d, q) in enumerate(pairs):
                cr, ci = carry[2 * n], carry[2 * n + 1]
                ir, ii = rows(d, q, j)
                ar, ai = coef(2 * d, q), coef(2 * d + 1, q)
                s_ref[ir] = cr
                s_ref[ii] = ci
                new += [ar * cr - ai * ci + w_ref[ir], ar * ci + ai * cr + w_ref[ii]]
            return tuple(new)

        totals = lax.fori_loop(0, seg, local, tuple(zero8 for _ in range(2 * len(pairs))))

        cins = []
        for n, (d, q) in enumerate(pairs):
            tr, ti = totals[2 * n], totals[2 * n + 1]
            sr, si = coef(4 + 2 * d, q), coef(5 + 2 * d, q)
            cr = jnp.zeros((1, 128), F32)
            ci = jnp.zeros((1, 128), F32)
            rr, ri = [None] * 8, [None] * 8
            for s in (range(8) if d == 0 else range(7, -1, -1)):
                rr[s], ri[s] = cr, ci
                cr, ci = sr * cr - si * ci + tr[s:s + 1], sr * ci + si * cr + ti[s:s + 1]
            cins += [jnp.concatenate(rr, axis=0), jnp.concatenate(ri, axis=0)]

        def fixup(j, ps):
            new = []
            for n, (d, q) in enumerate(pairs):
                pr, pi = ps[2 * n], ps[2 * n + 1]
                ir, ii = rows(d, q, j)
                cr, ci = cins[2 * n], cins[2 * n + 1]
                s_ref[ir] = s_ref[ir] + (pr * cr - pi * ci)
                s_ref[ii] = s_ref[ii] + (pr * ci + pi * cr)
                ar, ai = coef(2 * d, q), coef(2 * d + 1, q)
                new += [ar * pr - ai * pi, ar * pi + ai * pr]
            return tuple(new)

        lax.fori_loop(0, seg, fixup, tuple(one8 if n % 2 == 0 else zero8 for n in range(2 * len(pairs))))

    for bb in range(bsub):
        scan_sequence(bb * ncp)

    s_in = jnp.concatenate([s_ref[kk] for kk in range(4 * nq)], axis=1).astype(BF16)
    y = (_dot(xb, m_ref[...]) + _dot_nt(s_in, wout_ref[...])
         + x * jnp.concatenate([dsk_ref[0, 0]] * S5_T, axis=1))
    for bb in range(bsub):
        for t in range(S5_T):
            o_ref[bb, pl.ds(t, nc, stride=S5_T), :] = y[bb * ncp:bb * ncp + nc, t * blk:(t + 1) * blk]


def _s5(cols1, prm, lbr, lbi, ctr, cti, dsk, layer):
    bsz, lq, _ = cols1.shape
    nc = lq // S5_T
    ncp = -(-nc // 8) * 8
    ng = S5_WIDTH // 128
    n = S5_T * 128
    ublk = C1_U // 128
    bsub = 2 if (bsz % 2 == 0 and 2 * ncp <= 384) else 1
    return pl.pallas_call(
        functools.partial(_s5_kernel, nc=nc, bsub=bsub),
        grid=(ng, bsz // bsub),
        in_specs=[
            pl.BlockSpec((bsub, lq, 128), lambda g, b: (b, 0, ublk + g)),
            pl.BlockSpec((1, 1, 8, S5_NS), lambda g, b: (layer, g, 0, 0)),
            pl.BlockSpec((1, 1, 128, S5_NS), lambda g, b: (layer, g, 0, 0)),
            pl.BlockSpec((1, 1, 128, S5_NS), lambda g, b: (layer, g, 0, 0)),
            pl.BlockSpec((1, 2, 1, 128, S5_NS), lambda g, b: (layer, 0, g, 0, 0)),
            pl.BlockSpec((1, 2, 1, 128, S5_NS), lambda g, b: (layer, 0, g, 0, 0)),
            pl.BlockSpec((1, 1, 1, 128), lambda g, b: (layer, g, 0, 0)),
        ],
        out_specs=pl.BlockSpec((bsub, lq, 128), lambda g, b: (b, 0, g)),
        out_shape=jax.ShapeDtypeStruct((bsz, lq, S5_WIDTH), F32),
        scratch_shapes=[
            pltpu.VMEM((n, 4 * S5_NS), BF16),
            pltpu.VMEM((n, n), BF16),
            pltpu.VMEM((n, 4 * S5_NS), BF16),
            pltpu.VMEM(((2 * S5_T - 1) * 128, 128), F32),
            pltpu.VMEM((4 * S5_NS // 128, bsub * ncp, 128), F32),
            pltpu.VMEM((4 * S5_NS // 128, bsub * ncp, 128), F32),
            pltpu.VMEM((8, S5_NS), F32),
            pltpu.VMEM((bsub * ncp, n), F32),
        ],
        compiler_params=_params(("arbitrary", "arbitrary")),
        name="s5_mixer",
    )(cols1, prm, lbr, lbi, ctr, cti, dsk)


def _gla_kernel(*refs, rev, cpt, nt, pad):
    if rev:
        (q_ref, k_ref, v_ref, glr_ref, wg_ref, bg_ref, of_ref, gate_ref, ng_ref, o_ref, st_ref) = refs
    else:
        (q_ref, k_ref, v_ref, glr_ref, wg_ref, bg_ref, o_ref, st_ref) = refs
    i = pl.program_id(1)
    tile = nt - 1 - i if rev else i
    ch = GLA_CHUNK

    @pl.when(i == 0)
    def _():
        st_ref[...] = jnp.zeros_like(st_ref)

    tt = cpt * ch
    row = lax.broadcasted_iota(jnp.int32, (ch, ch), 0)
    col = lax.broadcasted_iota(jnp.int32, (ch, ch), 1)
    causal = (col >= row) if rev else (col <= row)
    tri = jnp.where(causal, 1.0, 0.0).astype(F32)

    glin = _dot(glr_ref[0].astype(BF16), wg_ref[0]) + bg_ref[0]
    g = jax.nn.log_sigmoid(glin) / GLA_GATE_NORM
    t_glob = tile * tt + lax.broadcasted_iota(jnp.int32, (tt, 1), 0)
    g = jnp.where(t_glob >= pad, g, 0.0)
    bcs = [jnp.dot(tri, g[c * ch:(c + 1) * ch], precision=HIGHEST, preferred_element_type=F32)
           for c in range(cpt)]
    blasts = [bc[0:1] if rev else bc[ch - 1:ch] for bc in bcs]
    bcum = jnp.concatenate(bcs, axis=0)
    brel = jnp.concatenate([blasts[c] - bcs[c] for c in range(cpt)], axis=0)
    k = k_ref[0].astype(F32)
    qe = (q_ref[0].astype(F32) * (GLA_DK ** -0.5) * jnp.exp(bcum)).astype(BF16)
    ke = (k * jnp.exp(-bcum)).astype(BF16)
    kd = (k * jnp.exp(brel)).astype(BF16)
    v = v_ref[0].astype(BF16)

    npair = GLA_HEADS // 2
    kw, vw = 2 * GLA_DK, 2 * GLA_DV
    hi = lambda shape, axis, shift: lax.broadcasted_iota(jnp.int32, shape, axis) >> shift
    bm_k = hi((2 * ch, kw), 0, 6) == hi((2 * ch, kw), 1, 6)
    bm_v = hi((2 * ch, vw), 0, 6) == hi((2 * ch, vw), 1, 7)
    bm_s = hi((vw, kw), 0, 7) == hi((vw, kw), 1, 6)
    arow = lax.broadcasted_iota(jnp.int32, (ch, 2 * ch), 0)
    acol = lax.broadcasted_iota(jnp.int32, (ch, 2 * ch), 1) & (ch - 1)
    causal2 = (acol >= arow) if rev else (acol <= arow)
    zk = jnp.zeros((), BF16)
    states = [st_ref[p] for p in range(npair)]
    for ci in range(cpt):
        c = cpt - 1 - ci if rev else ci
        rs = slice(c * ch, (c + 1) * ch)
        dec = jnp.exp(blasts[c])
        outs = []
        for p in range(npair):
            ks = slice(p * kw, (p + 1) * kw)
            vs = slice(p * vw, (p + 1) * vw)
            qp = qe[rs, ks]
            ke2 = jnp.where(bm_k, jnp.concatenate([ke[rs, ks]] * 2, axis=0), zk)
            v2 = jnp.where(bm_v, jnp.concatenate([v[rs, vs]] * 2, axis=0), zk)
            att = jnp.where(causal2, _dot_nt(qp, ke2), 0.0)
            outs.append(_dot(att.astype(BF16), v2) + _dot_nt(qp, states[p].astype(BF16)))
            states[p] = states[p] * dec[:, ks] + jnp.where(bm_s, _dot_tn(v[rs, vs], kd[rs, ks]), 0.0)
        o = jnp.concatenate(outs, axis=1)
        if rev:
            o = o + of_ref[0, rs, :]
            parts = []
            for h in range(GLA_HEADS):
                oh = o[:, h * GLA_DV:(h + 1) * GLA_DV]
                parts.append(oh * lax.rsqrt(jnp.mean(oh * oh, axis=-1, keepdims=True) + EPS))
            o = jnp.concatenate(parts, axis=1) * ng_ref[0]
            gate = gate_ref[0, rs, :].astype(F32)
            o = o * (gate * _sigmoid(gate))
        o_ref[0, rs, :] = o.astype(o_ref.dtype)
    for p in range(npair):
        st_ref[p] = states[p]


def _gla_pass(cols2, wg, bg, layer, rev, cpt, pad, o_fwd=None, norm_g=None):
    bsz, lq, _ = cols2.shape
    tt = cpt * GLA_CHUNK
    nt = lq // tt
    d = 1 if rev else 0
    tmap = (lambda i: nt - 1 - i) if rev else (lambda i: i)
    in_specs = [
        pl.BlockSpec((1, tt, GLA_KEY), lambda b, i: (b, tmap(i), C2_Q // GLA_KEY)),
        pl.BlockSpec((1, tt, GLA_KEY), lambda b, i: (b, tmap(i), C2_K // GLA_KEY)),
        pl.BlockSpec((1, tt, GLA_WIDTH), lambda b, i: (b, tmap(i), C2_V // GLA_WIDTH)),
        pl.BlockSpec((1, tt, 128), lambda b, i: (b, tmap(i), C2_GLR // 128)),
        pl.BlockSpec((1, 128, GLA_KEY), lambda b, i: (layer * 2 + d, 0, 0)),
        pl.BlockSpec((1, 1, GLA_KEY), lambda b, i: (layer * 2 + d, 0, 0)),
    ]
    args = [cols2, cols2, cols2, cols2, wg, bg]
    if rev:
        in_specs += [
            pl.BlockSpec((1, tt, GLA_WIDTH), lambda b, i: (b, tmap(i), 0)),
            pl.BlockSpec((1, tt, GLA_WIDTH), lambda b, i: (b, tmap(i), C2_GB // GLA_WIDTH)),
            pl.BlockSpec((1, 1, GLA_WIDTH), lambda b, i: (layer, 0, 0)),
        ]
        args += [o_fwd, cols2, norm_g]
    return pl.pallas_call(
        functools.partial(_gla_kernel, rev=rev, cpt=cpt, nt=nt, pad=pad),
        grid=(bsz, nt),
        in_specs=in_specs,
        out_specs=pl.BlockSpec((1, tt, GLA_WIDTH), lambda b, i: (b, tmap(i), 0)),
        out_shape=jax.ShapeDtypeStruct((bsz, lq, GLA_WIDTH), BF16 if rev else F32),
        scratch_shapes=[pltpu.VMEM((GLA_HEADS // 2, 2 * GLA_DV, 2 * GLA_DK), F32)],
        compiler_params=_params(("parallel", "arbitrary")),
        name="gla_bwd" if rev else "gla_fwd",
    )(*args)


def _lru_kernel(*refs, rev, tt, nt, pad):
    xs = refs[:LRU_BLOCKS]
    refs = refs[LRU_BLOCKS:]
    if rev:
        (xp_ref, xn_ref, cw_ref, cb_ref, w_ref, ba_ref, bx_ref, lam_ref, hf_ref) = refs[:9]
        gates = refs[9:9 + LRU_BLOCKS]
        o_ref, a_ref, b_ref, h_ref, on_ref = refs[9 + LRU_BLOCKS:]
    else:
        (xp_ref, xn_ref, cw_ref, cb_ref, w_ref, ba_ref, bx_ref, lam_ref,
         o_ref, a_ref, b_ref, h_ref) = refs
    i = pl.program_id(1)
    tile = nt - 1 - i if rev else i
    seg = tt // 8

    @pl.when(i == 0)
    def _():
        h_ref[...] = jnp.zeros_like(h_ref)

    def permuted(ref, lead):
        return jnp.concatenate([ref[lead + (pl.ds(j, 8, stride=seg), slice(None))] for j in range(seg)], axis=0)

    prev = jnp.where(tile == 0, 0.0, xp_ref[0])
    nxt = jnp.where(tile == nt - 1, 0.0, xn_ref[0])
    sub = lax.broadcasted_iota(jnp.int32, (8, LRU_BLOCK), 0)
    left = CONV_WIDTH // 2
    cw = cw_ref[0]
    sp = jax.nn.softplus(-lam_ref[0])
    rperm = lax.broadcasted_iota(jnp.int32, (tt, 1), 0)
    valid = (tile * tt + (rperm & 7) * seg + (rperm >> 3)) >= pad
    for n in range(LRU_BLOCKS):
        sl = slice(n * LRU_BLOCK, (n + 1) * LRU_BLOCK)
        xp = permuted(xs[n], (0,))
        before = []
        for k in range(left, 0, -1):
            edge = pltpu.roll(xp[(seg - k) * 8:(seg - k + 1) * 8], 1, axis=0)
            before.append(jnp.where(sub == 0, prev[HALO - k:HALO - k + 1, sl], edge))
        after = []
        for k in range(CONV_WIDTH - 1 - left):
            edge = pltpu.roll(xp[k * 8:(k + 1) * 8], 7, axis=0)
            after.append(jnp.where(sub == 7, nxt[k:k + 1, sl], edge))
        xext = jnp.concatenate(before + [xp] + after, axis=0)
        xcn = cb_ref[0][:, sl]
        for j in range(CONV_WIDTH):
            xcn = xcn + xext[8 * j:8 * j + tt] * cw[j:j + 1, sl]
        pre = _dot(xcn.astype(BF16), w_ref[0, n])
        tr = jnp.tanh(pre[:, :LRU_BLOCK] + 0.5 * ba_ref[0][:, sl])
        ig = 0.5 * jnp.tanh(pre[:, LRU_BLOCK:] + 0.5 * bx_ref[0][:, sl]) + 0.5
        c4 = (-0.5 * LRU_C) * sp[:, sl]
        log_a = c4 * tr + c4
        a = jnp.exp(log_a)
        one_m_a2 = -jnp.tanh(log_a) * (a * a + 1.0)
        root = jnp.where(one_m_a2 > 0.0, one_m_a2 * lax.rsqrt(one_m_a2), 0.0)
        a_ref[n] = a
        b_ref[n] = jnp.where(valid, root * (ig * xcn), 0.0)

    def vreg(j):
        jj = seg - 1 - j if rev else j
        return jj, pl.ds(pl.multiple_of(jj * 8, 8), 8)

    def local(j, carry):
        hs, ps = carry
        _, rows8 = vreg(j)
        nh, npr = [], []
        for n in range(LRU_BLOCKS):
            av = a_ref[n, rows8, :]
            h = av * hs[n] + b_ref[n, rows8, :]
            b_ref[n, rows8, :] = h
            nh.append(h)
            npr.append(av * ps[n])
        return tuple(nh), tuple(npr)

    zeros = tuple(jnp.zeros((8, LRU_BLOCK), F32) for _ in range(LRU_BLOCKS))
    ones = tuple(jnp.ones((8, LRU_BLOCK), F32) for _ in range(LRU_BLOCKS))
    h_end, p_end = lax.fori_loop(0, seg, local, (zeros, ones))

    cins = []
    for n in range(LRU_BLOCKS):
        c = h_ref[n]
        rows = [None] * 8
        for s in (range(7, -1, -1) if rev else range(8)):
            rows[s] = c
            c = p_end[n][s:s + 1] * c + h_end[n][s:s + 1]
        h_ref[n] = c
        cins.append(jnp.concatenate(rows, axis=0))

    def fixup(j, ps):
        jj, rows8 = vreg(j)
        strided = pl.ds(jj, 8, stride=seg)
        npr = []
        for n in range(LRU_BLOCKS):
            p = a_ref[n, rows8, :] * ps[n]
            h = b_ref[n, rows8, :] + p * cins[n]
            if rev:
                g = gates[n][0, strided, :]
                on_ref[n, strided, :] = (hf_ref[n, 0, strided, :] + h) * (g * _sigmoid(g))
            else:
                o_ref[n, 0, strided, :] = h
            npr.append(p)
        return tuple(npr)

    lax.fori_loop(0, seg, fixup, ones)
    if rev:
        for n in range(LRU_BLOCKS):
            o_ref[0, :, n * LRU_BLOCK:(n + 1) * LRU_BLOCK] = on_ref[n].astype(o_ref.dtype)


def _lru_pass(cols1, cw, cb, w, ba, bx, lam, layer, rev, tt, pad, h_fwd=None):
    bsz, lq, _ = cols1.shape
    nt = lq // tt
    d = 1 if rev else 0
    nh = lq // HALO
    th = tt // HALO
    tmap = (lambda i: nt - 1 - i) if rev else (lambda i: i)

    def slabs(col0):
        return [pl.BlockSpec((1, tt, LRU_BLOCK), functools.partial(lambda b, i, n: (b, tmap(i), n), n=col0 + n))
                for n in range(LRU_BLOCKS)]

    in_specs = slabs(C1_X // LRU_BLOCK) + [
        pl.BlockSpec((1, HALO, LRU_WIDTH), lambda b, i: (b, jnp.maximum(tmap(i) * th - 1, 0), 0)),
        pl.BlockSpec((1, HALO, LRU_WIDTH), lambda b, i: (b, jnp.minimum((tmap(i) + 1) * th, nh - 1), 0)),
        pl.BlockSpec((1, 8, LRU_WIDTH), lambda b, i: (layer, 0, 0)),
        pl.BlockSpec((1, 1, LRU_WIDTH), lambda b, i: (layer, 0, 0)),
        pl.BlockSpec((1, LRU_BLOCKS, LRU_BLOCK, 2 * LRU_BLOCK), lambda b, i: (layer * 2 + d, 0, 0, 0)),
        pl.BlockSpec((1, 1, LRU_WIDTH), lambda b, i: (layer * 2 + d, 0, 0)),
        pl.BlockSpec((1, 1, LRU_WIDTH), lambda b, i: (layer * 2 + d, 0, 0)),
        pl.BlockSpec((1, 1, LRU_WIDTH), lambda b, i: (layer * 2 + d, 0, 0)),
    ]
    args = [cols1] * (LRU_BLOCKS + 2) + [cw, cb, w, ba, bx, lam]
    scratch = [
        pltpu.VMEM((LRU_BLOCKS, tt, LRU_BLOCK), F32),
        pltpu.VMEM((LRU_BLOCKS, tt, LRU_BLOCK), F32),
        pltpu.VMEM((LRU_BLOCKS, 1, LRU_BLOCK), F32),
    ]
    if rev:
        in_specs += [pl.BlockSpec((LRU_BLOCKS, 1, tt, LRU_BLOCK), lambda b, i: (0, b, tmap(i), 0))]
        in_specs += slabs(C1_GC // LRU_BLOCK)
        args += [h_fwd] + [cols1] * LRU_BLOCKS
        scratch += [pltpu.VMEM((LRU_BLOCKS, tt, LRU_BLOCK), F32)]
        out_spec = pl.BlockSpec((1, tt, LRU_WIDTH), lambda b, i: (b, tmap(i), 0))
        out_shape = jax.ShapeDtypeStruct((bsz, lq, LRU_WIDTH), BF16)
    else:
        out_spec = pl.BlockSpec((LRU_BLOCKS, 1, tt, LRU_BLOCK), lambda b, i: (0, b, tmap(i), 0))
        out_shape = jax.ShapeDtypeStruct((LRU_BLOCKS, bsz, lq, LRU_BLOCK), F32)
    return pl.pallas_call(
        functools.partial(_lru_kernel, rev=rev, tt=tt, nt=nt, pad=pad),
        grid=(bsz, nt),
        in_specs=in_specs,
        out_specs=out_spec,
        out_shape=out_shape,
        scratch_shapes=scratch,
        compiler_params=_params(("parallel", "arbitrary")),
        name="lru_bwd" if rev else "lru_fwd",
    )(*args)


def _out_kernel(z_ref, g_ref, mask_ref, ys_ref, ga_ref, yb_ref, yc_ref, wglu_ref, bglu_ref,
                wm0_ref, wm1_ref, wm2_ref, woa_ref, wob_ref, woc_ref, wo_ref, fg_ref,
                o_ref, h_ref, ya_ref, *, final):
    j = pl.program_id(1)
    nj = pl.num_programs(1)

    @pl.when(j == 0)
    def _():
        z = z_ref[...]
        ms = jnp.mean(z * z, axis=-1, keepdims=True)
        h_ref[...] = (z * lax.rsqrt(ms + EPS) * g_ref[0]).astype(BF16)
        zz = jax.nn.gelu(ys_ref[...])
        glu = _sigmoid(_dot(zz.astype(BF16), wglu_ref[0]) + bglu_ref[0])
        ga = ga_ref[...]
        ya_ref[...] = (zz * glu * (ga * _sigmoid(ga))).astype(BF16)

    h = h_ref[...]
    m = (_sigmoid(_dot(h, wm0_ref[0])) * _dot(ya_ref[...], woa_ref[0])
         + _sigmoid(_dot(h, wm1_ref[0])) * _dot(yb_ref[...], wob_ref[0])
         + _sigmoid(_dot(h, wm2_ref[0])) * _dot(yc_ref[...], woc_ref[0]))
    contrib = _dot(m.astype(BF16), wo_ref[0])

    @pl.when(j == 0)
    def _():
        o_ref[...] = contrib

    @pl.when(jnp.logical_and(j > 0, j < nj - 1))
    def _():
        o_ref[...] += contrib

    @pl.when(j == nj - 1)
    def _():
        zn = z_ref[...] + (o_ref[...] + contrib)
        if final:
            ms = jnp.mean(zn * zn, axis=-1, keepdims=True)
            o_ref[...] = zn * lax.rsqrt(ms + EPS) * fg_ref[...]
        else:
            o_ref[...] = jnp.where(mask_ref[...] > 0.0, zn, 0.0)


def _out_proj(z, g, mask, cols1, ys, yb, yc, wglu, bglu, wm, woa, wob, woc, wo, fg, layer, tm, final):
    r, d = z.shape
    TJ = TJ_WIDE if tm <= 512 else TJ_NARROW
    nj = d // TJ
    return pl.pallas_call(
        functools.partial(_out_kernel, final=final),
        grid=(r // tm, nj),
        in_specs=[
            pl.BlockSpec((tm, d), lambda i, j: (i, 0)),
            pl.BlockSpec((1, 1, d), lambda i, j: (layer, 0, 0)),
            pl.BlockSpec((tm, 1), lambda i, j: (i, 0)),
            pl.BlockSpec((tm, S5_WIDTH), lambda i, j: (i, 0)),
            pl.BlockSpec((tm, S5_WIDTH), lambda i, j: (i, C1_GA // S5_WIDTH)),
            pl.BlockSpec((tm, GLA_WIDTH), lambda i, j: (i, 0)),
            pl.BlockSpec((tm, LRU_WIDTH), lambda i, j: (i, 0)),
            pl.BlockSpec((1, S5_WIDTH, S5_WIDTH), lambda i, j: (layer, 0, 0)),
            pl.BlockSpec((1, 1, S5_WIDTH), lambda i, j: (layer, 0, 0)),
            pl.BlockSpec((1, d, TJ), lambda i, j: (layer, 0, j)),
            pl.BlockSpec((1, d, TJ), lambda i, j: (layer, 0, nj + j)),
            pl.BlockSpec((1, d, TJ), lambda i, j: (layer, 0, 2 * nj + j)),
            pl.BlockSpec((1, S5_WIDTH, TJ), lambda i, j: (layer, 0, j)),
            pl.BlockSpec((1, GLA_WIDTH, TJ), lambda i, j: (layer, 0, j)),
            pl.BlockSpec((1, LRU_WIDTH, TJ), lambda i, j: (layer, 0, j)),
            pl.BlockSpec((1, TJ, d), lambda i, j: (layer, j, 0)),
            pl.BlockSpec((1, d), lambda i, j: (0, 0)),
        ],
        out_specs=pl.BlockSpec((tm, d), lambda i, j: (i, 0)),
        out_shape=jax.ShapeDtypeStruct((r, d), F32),
        scratch_shapes=[
            pltpu.VMEM((tm, d), BF16),
            pltpu.VMEM((tm, S5_WIDTH), BF16),
        ],
        compiler_params=_params(("parallel", "arbitrary")),
        name="out_proj",
    )(z, g, mask, ys, cols1, yb, yc, wglu, bglu, wm, wm, wm, woa, wob, woc, wo, fg)


def _prepare(p):
    depth = p["w_in"].shape[0]
    w_in = p["w_in"]
    o = 0
    seg = {}
    for name, width in (("u", 512), ("ga", 512), ("q", 256), ("k", 256), ("v", 512), ("gb", 512),
                        ("glr", 2 * GLA_RANK), ("x", 1024), ("gc", 1024), ("m", 3 * w_in.shape[1])):
        seg[name] = w_in[:, :, o:o + width]
        o += width
    w1 = jnp.concatenate([seg["x"], seg["gc"], seg["u"], seg["ga"]], axis=-1).astype(BF16)
    zpad = jnp.zeros(w_in.shape[:2] + (N2 - C2_GLR - 2 * GLA_RANK,), w_in.dtype)
    w2 = jnp.concatenate([seg["v"], seg["gb"], seg["q"], seg["k"], seg["glr"], zpad], axis=-1).astype(BF16)
    wm = seg["m"].astype(BF16)

    ng = S5_WIDTH // 128
    eye = jnp.eye(S5_GB, dtype=F32)

    def rows(a):
        return a.astype(F32).reshape(depth, 2, ng, S5_NS)

    ls = jnp.broadcast_to(p["s5_log_step"].astype(F32)[..., None], p["s5_lam_re"].shape)
    lr, li, ls = rows(p["s5_lam_re"]), rows(p["s5_lam_im"]), rows(ls)
    zero = jnp.zeros_like(lr[:, 0])
    prm = jnp.stack([lr[:, 0], li[:, 0], ls[:, 0], lr[:, 1], li[:, 1], ls[:, 1], zero, zero], axis=2)

    def emb_b(bm):
        bm = bm.astype(F32).reshape(depth, ng, S5_GB, 64, 16)
        return jnp.einsum("lGgnc,gh->lGgchn", bm, eye).reshape(depth, ng, 128, S5_NS)

    def emb_c(cm):
        cm = cm.astype(F32).reshape(depth, 2, ng, S5_GB, 16, 64)
        return jnp.einsum("ldGgcn,gh->ldGgchn", cm, eye).reshape(depth, 2, ng, 128, S5_NS)

    wg = jnp.zeros((depth, 2, 128, GLA_KEY), F32)
    for d in range(2):
        wg = wg.at[:, d, d * GLA_RANK:(d + 1) * GLA_RANK, :].set(p["gla_w_gate_up"][:, d].astype(F32))
    lru_w = (0.5 * jnp.concatenate([p["lru_w_a"], p["lru_w_x"]], axis=-1)).astype(BF16)
    cw = jnp.concatenate([p["conv_w"].astype(F32),
                          jnp.zeros((depth, 8 - CONV_WIDTH, LRU_WIDTH), F32)], axis=1)
    return dict(
        norm_g=p["norm_g"].astype(F32).reshape(depth, 1, -1), w1=w1, w2=w2, wm=wm,
        s5_prm=prm, s5_lbr=emb_b(p["s5_b_re"]), s5_lbi=emb_b(p["s5_b_im"]),
        s5_ctr=emb_c(p["s5_c_re"]), s5_cti=emb_c(p["s5_c_im"]),
        s5_d=p["s5_d"].astype(F32).reshape(depth, ng, 1, 128),
        wglu=p["s5_w_glu"].astype(BF16), bglu=p["s5_b_glu"].astype(F32).reshape(depth, 1, S5_WIDTH),
        wg=wg.reshape(depth * 2, 128, GLA_KEY).astype(BF16),
        bg=p["gla_b_gate"].astype(F32).reshape(depth * 2, 1, GLA_KEY),
        gla_ng=p["gla_norm_g"].astype(F32).reshape(depth, 1, GLA_WIDTH),
        cw=cw, cb=p["conv_b"].astype(F32).reshape(depth, 1, LRU_WIDTH),
        lru_w=lru_w.reshape(depth * 2, LRU_BLOCKS, LRU_BLOCK, 2 * LRU_BLOCK),
        lru_ba=p["lru_b_a"].astype(F32).reshape(depth * 2, 1, LRU_WIDTH),
        lru_bx=p["lru_b_x"].astype(F32).reshape(depth * 2, 1, LRU_WIDTH),
        lru_lam=p["lru_lam"].astype(F32).reshape(depth * 2, 1, LRU_WIDTH),
        woa=p["w_out_a"].astype(BF16), wob=p["w_out_b"].astype(BF16), woc=p["w_out_c"].astype(BF16),
        wo=p["w_o"].astype(BF16), fg=p["final_norm_g"].astype(F32).reshape(1, -1),
    )


def _pick_tile(n, unit, target):
    best = unit
    for t in range(unit, target + 1, unit):
        if n % t == 0:
            best = t
    return best


def _pick_lru_tile(lq, target):
    best = None
    for seg in range(1, target // 8 + 1):
        if lq % (8 * seg) == 0 and seg % 8 != 0:
            best = 8 * seg
    assert best is not None
    return best


def _encoder(x, meta, w):
    bsz, seq, d = x.shape
    depth = w["w2"].shape[0]
    pad = PAD
    while (bsz * (pad + N_META + seq)) % 512 != 0 and pad < PAD + 4 * GLA_CHUNK:
        pad += GLA_CHUNK
    if (bsz * (pad + N_META + seq)) % 512 != 0:
        pad = PAD
    lq = pad + N_META + seq
    r = bsz * lq
    head = jnp.concatenate([jnp.zeros((pad, d), F32), meta.astype(F32)], axis=0)
    z = jnp.concatenate([jnp.broadcast_to(head[None], (bsz, pad + N_META, d)), x.astype(F32)], axis=1)
    z = z.reshape(r, d)
    mask = jnp.broadcast_to((jnp.arange(lq) >= pad).astype(F32)[None, :, None], (bsz, lq, 1)).reshape(r, 1)
    tm = _pick_tile(r, 128, 768)
    tm_out = _pick_tile(r, 128, 640)
    cpt = _pick_tile(lq // GLA_CHUNK, 1, 17)
    tt = _pick_lru_tile(lq, 640)
    for l in range(depth):
        cols1, cols2 = _in_proj(z, w["norm_g"], w["w1"], w["w2"], l, tm)
        c1 = cols1.reshape(bsz, lq, N1)
        c2 = cols2.reshape(bsz, lq, N2)
        ys = _s5(c1, w["s5_prm"], w["s5_lbr"], w["s5_lbi"], w["s5_ctr"], w["s5_cti"], w["s5_d"], l)
        of = _gla_pass(c2, w["wg"], w["bg"], l, False, cpt, pad)
        yb = _gla_pass(c2, w["wg"], w["bg"], l, True, cpt, pad, of, w["gla_ng"])
        lru_w = (w["cw"], w["cb"], w["lru_w"], w["lru_ba"], w["lru_bx"], w["lru_lam"])
        hf = _lru_pass(c1, *lru_w, l, False, tt, pad)
        yc = _lru_pass(c1, *lru_w, l, True, tt, pad, hf)
        z = _out_proj(z, w["norm_g"], mask, cols1, ys.reshape(r, -1), yb.reshape(r, -1),
                      yc.reshape(r, -1), w["wglu"], w["bglu"], w["wm"], w["woa"], w["wob"], w["woc"],
                      w["wo"], w["fg"], l, tm_out, l == depth - 1)
    return z.reshape(bsz, lq, d)[:, pad + N_META:].astype(x.dtype)


def kernel(x_prompt, x_sample, meta_tokens, norm_g, w_in, s5_lam_re, s5_lam_im, s5_log_step, s5_b_re, s5_b_im, s5_c_re, s5_c_im, s5_d, s5_w_glu, s5_b_glu, gla_w_gate_up, gla_b_gate, gla_norm_g, conv_w, conv_b, lru_w_a, lru_b_a, lru_w_x, lru_b_x, lru_lam, w_out_a, w_out_b, w_out_c, w_o, final_norm_g):
    w = _prepare(dict(
        norm_g=norm_g, w_in=w_in, s5_lam_re=s5_lam_re, s5_lam_im=s5_lam_im, s5_log_step=s5_log_step,
        s5_b_re=s5_b_re, s5_b_im=s5_b_im, s5_c_re=s5_c_re, s5_c_im=s5_c_im, s5_d=s5_d,
        s5_w_glu=s5_w_glu, s5_b_glu=s5_b_glu, gla_w_gate_up=gla_w_gate_up, gla_b_gate=gla_b_gate,
        gla_norm_g=gla_norm_g, conv_w=conv_w, conv_b=conv_b, lru_w_a=lru_w_a, lru_b_a=lru_b_a,
        lru_w_x=lru_w_x, lru_b_x=lru_b_x, lru_lam=lru_lam, w_out_a=w_out_a, w_out_b=w_out_b,
        w_out_c=w_out_c, w_o=w_o, final_norm_g=final_norm_g))
    return (_encoder(x_prompt, meta_tokens, w), _encoder(x_sample, meta_tokens, w))
```

```python
import functools

import jax
import jax.numpy as jnp
from jax import lax
from jax.experimental import pallas as pl
from jax.experimental.pallas import tpu as pltpu

F32 = jnp.float32
BF16 = jnp.bfloat16
HIGHEST = lax.Precision.HIGHEST

N_META = 16
PAD = 48
EPS = 1e-6
S5_WIDTH = 512
S5_T = 8
S5_GB = 8
S5_NS = 512
GLA_HEADS = 4
GLA_DK = 64
GLA_DV = 128
GLA_KEY = 256
GLA_WIDTH = 512
GLA_RANK = 16
GLA_GATE_NORM = 16.0
GLA_CHUNK = 64
LRU_WIDTH = 1024
LRU_BLOCKS = 8
LRU_BLOCK = 128
CONV_WIDTH = 4
LRU_C = 8.0
HALO = 8
LANES = 128
VMEM_LIMIT = 56 * 1024 * 1024
IN_TILE_MAX = 768
OUT_TILE_MAX = 640
OUT_TILE = 512
GLA_CHUNKS_MAX = 17
LRU_TILE_MAX = 640

C1_X, C1_GC, C1_U, C1_GA = 0, 1024, 2048, 2560
N1 = 3072
C2_V, C2_GB, C2_Q, C2_K, C2_GLR = 0, 512, 1024, 1280, 1536
N2 = 1664
TN = 1024
NJ1 = N1 // TN
TJ_WIDE, TJ_NARROW = 512, 256


def _dot(a, b):
    return jnp.dot(a, b, preferred_element_type=F32)


def _dot_nt(a, b):
    return lax.dot_general(a, b, (((1,), (1,)), ((), ())), preferred_element_type=F32)


def _dot_tn(a, b):
    return lax.dot_general(a, b, (((0,), (0,)), ((), ())), preferred_element_type=F32)


def _sigmoid(x):
    return 0.5 * jnp.tanh(0.5 * x) + 0.5


def _params(sem):
    return pltpu.CompilerParams(dimension_semantics=sem, vmem_limit_bytes=VMEM_LIMIT)


def _kin_kernel(z_ref, g_ref, w1_ref, w2_ref, o1_ref, o2_ref, h_ref):
    j = pl.program_id(1)

    @pl.when(j == 0)
    def _():
        z = z_ref[...]
        ms = jnp.mean(z * z, axis=-1, keepdims=True)
        h_ref[...] = (z * lax.rsqrt(ms + EPS) * g_ref[0]).astype(BF16)

    @pl.when(j < NJ1)
    def _():
        o1_ref[...] = _dot(h_ref[...], w1_ref[0])

    @pl.when(j == NJ1)
    def _():
        o2_ref[...] = _dot(h_ref[...], w2_ref[0]).astype(o2_ref.dtype)


def _in_proj(z, g, w1, w2, layer, tm):
    r, d = z.shape
    last = NJ1 - 1
    return pl.pallas_call(
        _kin_kernel,
        grid=(r // tm, NJ1 + 1),
        in_specs=[
            pl.BlockSpec((tm, d), lambda i, j: (i, 0)),
            pl.BlockSpec((1, 1, d), lambda i, j: (layer, 0, 0)),
            pl.BlockSpec((1, d, TN), lambda i, j: (layer, 0, jnp.minimum(j, last))),
            pl.BlockSpec((1, d, N2), lambda i, j: (layer, 0, 0)),
        ],
        out_specs=[
            pl.BlockSpec((tm, TN), lambda i, j: (i, jnp.minimum(j, last))),
            pl.BlockSpec((tm, N2), lambda i, j: (i, 0)),
        ],
        out_shape=[jax.ShapeDtypeStruct((r, N1), F32), jax.ShapeDtypeStruct((r, N2), BF16)],
        scratch_shapes=[pltpu.VMEM((tm, d), BF16)],
        compiler_params=_params(("parallel", "arbitrary")),
        name="in_proj",
    )(z, g, w1, w2)


def _s5_kernel(u_ref, prm_ref, lbr_ref, lbi_ref, ctr_ref, cti_ref, dsk_ref, o_ref,
               wst_ref, m_ref, wout_ref, e_ref, w_ref, s_ref, apow_ref, x_ref, *, nc, bsub):
    b = pl.program_id(1)
    ns = S5_NS
    blk = S5_GB * 16
    ncp = -(-nc // 8) * 8

    @pl.when(b == 0)
    def _build():
        prm = prm_ref[0, 0]
        evec = lax.broadcasted_iota(jnp.int32, (24, 1), 0).astype(F32)
        e_ref[...] = jnp.zeros_like(e_ref)
        for d in range(2):
            lr = prm[3 * d:3 * d + 1]
            li = prm[3 * d + 1:3 * d + 2]
            dt = jnp.exp(prm[3 * d + 2:3 * d + 3])
            mag = jnp.exp(evec * (lr * dt))
            ang = evec * (li * dt)
            pre = mag * jnp.cos(ang)
            pim = mag * jnp.sin(ang)
            abr = pre[1:2]
            abi = pim[1:2]
            den = lr * lr + li * li
            fr = ((abr - 1.0) * lr + abi * li) / den
            fi = (abi * lr - (abr - 1.0) * li) / den
            lbr = lbr_ref[0, 0]
            lbi = lbi_ref[0, 0]
            bbr = fr * lbr - fi * lbi
            bbi = fr * lbi + fi * lbr
            ctr = ctr_ref[0, d, 0]
            cti = cti_ref[0, d, 0]
            apow_ref[2 * d:2 * d + 1, :] = pre[S5_T:S5_T + 1]
            apow_ref[2 * d + 1:2 * d + 2, :] = pim[S5_T:S5_T + 1]
            sr, si = jnp.ones_like(lr), jnp.zeros_like(lr)
            br_, bi_ = pre[S5_T:S5_T + 1], pim[S5_T:S5_T + 1]
            nseg = ncp // 8
            while nseg:
                if nseg & 1:
                    sr, si = sr * br_ - si * bi_, sr * bi_ + si * br_
                br_, bi_ = br_ * br_ - bi_ * bi_, 2.0 * br_ * bi_
                nseg >>= 1
            apow_ref[4 + 2 * d:5 + 2 * d, :] = sr
            apow_ref[5 + 2 * d:6 + 2 * d, :] = si
            for r in range(S5_T):
                e = S5_T - 1 - r if d == 0 else r
                pr_e = pre[e:e + 1]
                pi_e = pim[e:e + 1]
                l_re = bbr * pr_e - bbi * pi_e
                l_im = bbr * pi_e + bbi * pr_e
                base = 2 * d * ns
                wst_ref[r * blk:(r + 1) * blk, base:base + ns] = l_re.astype(BF16)
                wst_ref[r * blk:(r + 1) * blk, base + ns:base + 2 * ns] = l_im.astype(BF16)
                kblk = _dot_nt(l_re.astype(BF16), ctr.astype(BF16)) - _dot_nt(l_im.astype(BF16), cti.astype(BF16))
                eb = r if d == 0 else S5_T - 1 + r
                e_ref[eb * blk:(eb + 1) * blk, :] += kblk
                eo = r + 1 if d == 0 else S5_T - r
                pr_o = pre[eo:eo + 1]
                pi_o = pim[eo:eo + 1]
                wout_ref[r * blk:(r + 1) * blk, base:base + ns] = (pr_o * ctr - pi_o * cti).astype(BF16)
                wout_ref[r * blk:(r + 1) * blk, base + ns:base + 2 * ns] = (
                    -(pi_o * ctr) - pr_o * cti).astype(BF16)
        for t in range(S5_T):
            off = (S5_T - 1 - t) * blk
            m_ref[:, t * blk:(t + 1) * blk] = e_ref[off:off + S5_T * blk, :].astype(BF16)

    for bb in range(bsub):
        for t in range(S5_T):
            x_ref[bb * ncp:bb * ncp + nc, t * blk:(t + 1) * blk] = u_ref[bb, pl.ds(t, nc, stride=S5_T), :]
        if ncp > nc:
            x_ref[bb * ncp + nc:(bb + 1) * ncp, :] = jnp.zeros((ncp - nc, S5_T * blk), F32)
    x = x_ref[...]
    xb = x.astype(BF16)
    wv = _dot(xb, wst_ref[...])
    nq = ns // 128
    for kk in range(4 * nq):
        w_ref[kk] = wv[:, kk * 128:(kk + 1) * 128]

    seg = ncp // 8
    apow = apow_ref[...]
    pairs = [(d, q) for d in range(2) for q in range(nq)]

    def coef(row, q):
        return apow[row:row + 1, q * 128:(q + 1) * 128]

    zero8 = jnp.zeros((8, 128), F32)
    one8 = jnp.ones((8, 128), F32)

    def scan_sequence(base):
        def rows(d, q, j):
            jj = base + (j if d == 0 else seg - 1 - j)
            return ((2 * d * nq + q, pl.ds(jj, 8, stride=seg), slice(None)),
                    ((2 * d + 1) * nq + q, pl.ds(jj, 8, stride=seg), slice(None)))

        def local(j, carry):
            new = []
            for n, (d, q) in enumerate(pairs):
                cr, ci = carry[2 * n], carry[2 * n + 1]
                ir, ii = rows(d, q, j)
                ar, ai = coef(2 * d, q), coef(2 * d + 1, q)
                s_ref[ir] = cr
                s_ref[ii] = ci
                new += [ar * cr - ai * ci + w_ref[ir], ar * ci + ai * cr + w_ref[ii]]
            return tuple(new)

        totals = lax.fori_loop(0, seg, local, tuple(zero8 for _ in range(2 * len(pairs))))

        cins = []
        for n, (d, q) in enumerate(pairs):
            tr, ti = totals[2 * n], totals[2 * n + 1]
            sr, si = coef(4 + 2 * d, q), coef(5 + 2 * d, q)
            cr = jnp.zeros((1, 128), F32)
            ci = jnp.zeros((1, 128), F32)
            rr, ri = [None] * 8, [None] * 8
            for s in (range(8) if d == 0 else range(7, -1, -1)):
                rr[s], ri[s] = cr, ci
                cr, ci = sr * cr - si * ci + tr[s:s + 1], sr * ci + si * cr + ti[s:s + 1]
            cins += [jnp.concatenate(rr, axis=0), jnp.concatenate(ri, axis=0)]

        def fixup(j, ps):
            new = []
            for n, (d, q) in enumerate(pairs):
                pr, pi = ps[2 * n], ps[2 * n + 1]
                ir, ii = rows(d, q, j)
                cr, ci = cins[2 * n], cins[2 * n + 1]
                s_ref[ir] = s_ref[ir] + (pr * cr - pi * ci)
                s_ref[ii] = s_ref[ii] + (pr * ci + pi * cr)
                ar, ai = coef(2 * d, q), coef(2 * d + 1, q)
                new += [ar * pr - ai * pi, ar * pi + ai * pr]
            return tuple(new)

        lax.fori_loop(0, seg, fixup, tuple(one8 if n % 2 == 0 else zero8 for n in range(2 * len(pairs))))

    for bb in range(bsub):
        scan_sequence(bb * ncp)

    s_in = jnp.concatenate([s_ref[kk] for kk in range(4 * nq)], axis=1).astype(BF16)
    y = (_dot(xb, m_ref[...]) + _dot_nt(s_in, wout_ref[...])
         + x * jnp.concatenate([dsk_ref[0, 0]] * S5_T, axis=1))
    for bb in range(bsub):
        for t in range(S5_T):
            o_ref[bb, pl.ds(t, nc, stride=S5_T), :] = y[bb * ncp:bb * ncp + nc, t * blk:(t + 1) * blk]


def _s5(cols1, prm, lbr, lbi, ctr, cti, dsk, layer):
    bsz, lq, _ = cols1.shape
    nc = lq // S5_T
    ncp = -(-nc // 8) * 8
    ng = S5_WIDTH // LANES
    n = S5_T * LANES
    ublk = C1_U // LANES
    bsub = 2 if (bsz % 2 == 0 and 2 * ncp <= 640) else 1
    return pl.pallas_call(
        functools.partial(_s5_kernel, nc=nc, bsub=bsub),
        grid=(ng, bsz // bsub),
        in_specs=[
            pl.BlockSpec((bsub, lq, 128), lambda g, b: (b, 0, ublk + g)),
            pl.BlockSpec((1, 1, 8, S5_NS), lambda g, b: (layer, g, 0, 0)),
            pl.BlockSpec((1, 1, 128, S5_NS), lambda g, b: (layer, g, 0, 0)),
            pl.BlockSpec((1, 1, 128, S5_NS), lambda g, b: (layer, g, 0, 0)),
            pl.BlockSpec((1, 2, 1, 128, S5_NS), lambda g, b: (layer, 0, g, 0, 0)),
            pl.BlockSpec((1, 2, 1, 128, S5_NS), lambda g, b: (layer, 0, g, 0, 0)),
            pl.BlockSpec((1, 1, 1, 128), lambda g, b: (layer, g, 0, 0)),
        ],
        out_specs=pl.BlockSpec((bsub, lq, 128), lambda g, b: (b, 0, g)),
        out_shape=jax.ShapeDtypeStruct((bsz, lq, S5_WIDTH), F32),
        scratch_shapes=[
            pltpu.VMEM((n, 4 * S5_NS), BF16),
            pltpu.VMEM((n, n), BF16),
            pltpu.VMEM((n, 4 * S5_NS), BF16),
            pltpu.VMEM(((2 * S5_T - 1) * 128, 128), F32),
            pltpu.VMEM((4 * S5_NS // 128, bsub * ncp, 128), F32),
            pltpu.VMEM((4 * S5_NS // 128, bsub * ncp, 128), F32),
            pltpu.VMEM((8, S5_NS), F32),
            pltpu.VMEM((bsub * ncp, n), F32),
        ],
        compiler_params=_params(("arbitrary", "arbitrary")),
        name="s5_mixer",
    )(cols1, prm, lbr, lbi, ctr, cti, dsk)


def _gla_kernel(*refs, rev, cpt, nt, pad):
    if rev:
        (q_ref, k_ref, v_ref, glr_ref, wg_ref, bg_ref, of_ref, gate_ref, ng_ref, o_ref, st_ref) = refs
    else:
        (q_ref, k_ref, v_ref, glr_ref, wg_ref, bg_ref, o_ref, st_ref) = refs
    i = pl.program_id(1)
    tile = nt - 1 - i if rev else i
    ch = GLA_CHUNK

    @pl.when(i == 0)
    def _():
        st_ref[...] = jnp.zeros_like(st_ref)

    tt = cpt * ch
    row = lax.broadcasted_iota(jnp.int32, (ch, ch), 0)
    col = lax.broadcasted_iota(jnp.int32, (ch, ch), 1)
    causal = (col >= row) if rev else (col <= row)
    tri = jnp.where(causal, 1.0, 0.0).astype(F32)

    glin = _dot(glr_ref[0].astype(BF16), wg_ref[0]) + bg_ref[0]
    g = jax.nn.log_sigmoid(glin) / GLA_GATE_NORM
    t_glob = tile * tt + lax.broadcasted_iota(jnp.int32, (tt, 1), 0)
    g = jnp.where(t_glob >= pad, g, 0.0)
    bcs = [jnp.dot(tri, g[c * ch:(c + 1) * ch], precision=HIGHEST, preferred_element_type=F32)
           for c in range(cpt)]
    blasts = [bc[0:1] if rev else bc[ch - 1:ch] for bc in bcs]
    bcum = jnp.concatenate(bcs, axis=0)
    brel = jnp.concatenate([blasts[c] - bcs[c] for c in range(cpt)], axis=0)
    k = k_ref[0].astype(F32)
    qe = (q_ref[0].astype(F32) * (GLA_DK ** -0.5) * jnp.exp(bcum)).astype(BF16)
    ke = (k * jnp.exp(-bcum)).astype(BF16)
    kd = (k * jnp.exp(brel)).astype(BF16)
    v = v_ref[0].astype(BF16)

    npair = GLA_HEADS // 2
    kw, vw = 2 * GLA_DK, 2 * GLA_DV
    hi = lambda shape, axis, shift: lax.broadcasted_iota(jnp.int32, shape, axis) >> shift
    bm_k = hi((2 * ch, kw), 0, 6) == hi((2 * ch, kw), 1, 6)
    bm_v = hi((2 * ch, vw), 0, 6) == hi((2 * ch, vw), 1, 7)
    bm_s = hi((vw, kw), 0, 7) == hi((vw, kw), 1, 6)
    arow = lax.broadcasted_iota(jnp.int32, (ch, 2 * ch), 0)
    acol = lax.broadcasted_iota(jnp.int32, (ch, 2 * ch), 1) & (ch - 1)
    causal2 = (acol >= arow) if rev else (acol <= arow)
    zk = jnp.zeros((), BF16)
    states = [st_ref[p] for p in range(npair)]
    for ci in range(cpt):
        c = cpt - 1 - ci if rev else ci
        rs = slice(c * ch, (c + 1) * ch)
        dec = jnp.exp(blasts[c])
        outs = []
        for p in range(npair):
            ks = slice(p * kw, (p + 1) * kw)
            vs = slice(p * vw, (p + 1) * vw)
            qp = qe[rs, ks]
            ke2 = jnp.where(bm_k, jnp.concatenate([ke[rs, ks]] * 2, axis=0), zk)
            v2 = jnp.where(bm_v, jnp.concatenate([v[rs, vs]] * 2, axis=0), zk)
            att = jnp.where(causal2, _dot_nt(qp, ke2), 0.0)
            outs.append(_dot(att.astype(BF16), v2) + _dot_nt(qp, states[p].astype(BF16)))
            states[p] = states[p] * dec[:, ks] + jnp.where(bm_s, _dot_tn(v[rs, vs], kd[rs, ks]), 0.0)
        o = jnp.concatenate(outs, axis=1)
        if rev:
            o = o + of_ref[0, rs, :]
            parts = []
            for h in range(GLA_HEADS):
                oh = o[:, h * GLA_DV:(h + 1) * GLA_DV]
                parts.append(oh * lax.rsqrt(jnp.mean(oh * oh, axis=-1, keepdims=True) + EPS))
            o = jnp.concatenate(parts, axis=1) * ng_ref[0]
            gate = gate_ref[0, rs, :].astype(F32)
            o = o * (gate * _sigmoid(gate))
        o_ref[0, rs, :] = o.astype(o_ref.dtype)
    for p in range(npair):
        st_ref[p] = states[p]


def _gla_pass(cols2, wg, bg, layer, rev, cpt, pad, o_fwd=None, norm_g=None):
    bsz, lq, _ = cols2.shape
    tt = cpt * GLA_CHUNK
    nt = lq // tt
    d = 1 if rev else 0
    tmap = (lambda i: nt - 1 - i) if rev else (lambda i: i)
    in_specs = [
        pl.BlockSpec((1, tt, GLA_KEY), lambda b, i: (b, tmap(i), C2_Q // GLA_KEY)),
        pl.BlockSpec((1, tt, GLA_KEY), lambda b, i: (b, tmap(i), C2_K // GLA_KEY)),
        pl.BlockSpec((1, tt, GLA_WIDTH), lambda b, i: (b, tmap(i), C2_V // GLA_WIDTH)),
        pl.BlockSpec((1, tt, 128), lambda b, i: (b, tmap(i), C2_GLR // 128)),
        pl.BlockSpec((1, 128, GLA_KEY), lambda b, i: (layer * 2 + d, 0, 0)),
        pl.BlockSpec((1, 1, GLA_KEY), lambda b, i: (layer * 2 + d, 0, 0)),
    ]
    args = [cols2, cols2, cols2, cols2, wg, bg]
    if rev:
        in_specs += [
            pl.BlockSpec((1, tt, GLA_WIDTH), lambda b, i: (b, tmap(i), 0)),
            pl.BlockSpec((1, tt, GLA_WIDTH), lambda b, i: (b, tmap(i), C2_GB // GLA_WIDTH)),
            pl.BlockSpec((1, 1, GLA_WIDTH), lambda b, i: (layer, 0, 0)),
        ]
        args += [o_fwd, cols2, norm_g]
    return pl.pallas_call(
        functools.partial(_gla_kernel, rev=rev, cpt=cpt, nt=nt, pad=pad),
        grid=(bsz, nt),
        in_specs=in_specs,
        out_specs=pl.BlockSpec((1, tt, GLA_WIDTH), lambda b, i: (b, tmap(i), 0)),
        out_shape=jax.ShapeDtypeStruct((bsz, lq, GLA_WIDTH), BF16 if rev else F32),
        scratch_shapes=[pltpu.VMEM((GLA_HEADS // 2, 2 * GLA_DV, 2 * GLA_DK), F32)],
        compiler_params=_params(("parallel", "arbitrary")),
        name="gla_bwd" if rev else "gla_fwd",
    )(*args)


def _lru_kernel(*refs, rev, tt, nt, pad):
    xs = refs[:LRU_BLOCKS]
    refs = refs[LRU_BLOCKS:]
    if rev:
        (xp_ref, xn_ref, cw_ref, cb_ref, w_ref, ba_ref, bx_ref, lam_ref, hf_ref) = refs[:9]
        gates = refs[9:9 + LRU_BLOCKS]
        o_ref, a_ref, b_ref, h_ref, on_ref = refs[9 + LRU_BLOCKS:]
    else:
        (xp_ref, xn_ref, cw_ref, cb_ref, w_ref, ba_ref, bx_ref, lam_ref,
         o_ref, a_ref, b_ref, h_ref) = refs
    i = pl.program_id(1)
    tile = nt - 1 - i if rev else i
    seg = tt // 8

    @pl.when(i == 0)
    def _():
        h_ref[...] = jnp.zeros_like(h_ref)

    def permuted(ref, lead):
        return jnp.concatenate([ref[lead + (pl.ds(j, 8, stride=seg), slice(None))] for j in range(seg)], axis=0)

    prev = jnp.where(tile == 0, 0.0, xp_ref[0])
    nxt = jnp.where(tile == nt - 1, 0.0, xn_ref[0])
    sub = lax.broadcasted_iota(jnp.int32, (8, LRU_BLOCK), 0)
    left = CONV_WIDTH // 2
    cw = cw_ref[0]
    sp = jax.nn.softplus(-lam_ref[0])
    rperm = lax.broadcasted_iota(jnp.int32, (tt, 1), 0)
    valid = (tile * tt + (rperm & 7) * seg + (rperm >> 3)) >= pad
    for n in range(LRU_BLOCKS):
        sl = slice(n * LRU_BLOCK, (n + 1) * LRU_BLOCK)
        xp = permuted(xs[n], (0,))
        before = []
        for k in range(left, 0, -1):
            edge = pltpu.roll(xp[(seg - k) * 8:(seg - k + 1) * 8], 1, axis=0)
            before.append(jnp.where(sub == 0, prev[HALO - k:HALO - k + 1, sl], edge))
        after = []
        for k in range(CONV_WIDTH - 1 - left):
            edge = pltpu.roll(xp[k * 8:(k + 1) * 8], 7, axis=0)
            after.append(jnp.where(sub == 7, nxt[k:k + 1, sl], edge))
        xext = jnp.concatenate(before + [xp] + after, axis=0)
        xcn = cb_ref[0][:, sl]
        for j in range(CONV_WIDTH):
            xcn = xcn + xext[8 * j:8 * j + tt] * cw[j:j + 1, sl]
        pre = _dot(xcn.astype(BF16), w_ref[0, n])
        tr = jnp.tanh(pre[:, :LRU_BLOCK] + 0.5 * ba_ref[0][:, sl])
        ig = 0.5 * jnp.tanh(pre[:, LRU_BLOCK:] + 0.5 * bx_ref[0][:, sl]) + 0.5
        c4 = (-0.5 * LRU_C) * sp[:, sl]
        log_a = c4 * tr + c4
        a = jnp.exp(log_a)
        one_m_a2 = -jnp.tanh(log_a) * (a * a + 1.0)
        root = jnp.where(one_m_a2 > 0.0, one_m_a2 * lax.rsqrt(one_m_a2), 0.0)
        a_ref[n] = a
        b_ref[n] = jnp.where(valid, root * (ig * xcn), 0.0)

    def vreg(j):
        jj = seg - 1 - j if rev else j
        return jj, pl.ds(pl.multiple_of(jj * 8, 8), 8)

    def local(j, carry):
        hs, ps = carry
        _, rows8 = vreg(j)
        nh, npr = [], []
        for n in range(LRU_BLOCKS):
            av = a_ref[n, rows8, :]
            h = av * hs[n] + b_ref[n, rows8, :]
            b_ref[n, rows8, :] = h
            nh.append(h)
            npr.append(av * ps[n])
        return tuple(nh), tuple(npr)

    zeros = tuple(jnp.zeros((8, LRU_BLOCK), F32) for _ in range(LRU_BLOCKS))
    ones = tuple(jnp.ones((8, LRU_BLOCK), F32) for _ in range(LRU_BLOCKS))
    h_end, p_end = lax.fori_loop(0, seg, local, (zeros, ones))

    cins = []
    for n in range(LRU_BLOCKS):
        c = h_ref[n]
        rows = [None] * 8
        for s in (range(7, -1, -1) if rev else range(8)):
            rows[s] = c
            c = p_end[n][s:s + 1] * c + h_end[n][s:s + 1]
        h_ref[n] = c
        cins.append(jnp.concatenate(rows, axis=0))

    def fixup(j, ps):
        jj, rows8 = vreg(j)
        strided = pl.ds(jj, 8, stride=seg)
        npr = []
        for n in range(LRU_BLOCKS):
            p = a_ref[n, rows8, :] * ps[n]
            h = b_ref[n, rows8, :] + p * cins[n]
            if rev:
                g = gates[n][0, strided, :]
                on_ref[n, strided, :] = (hf_ref[n, 0, strided, :] + h) * (g * _sigmoid(g))
            else:
                o_ref[n, 0, strided, :] = h
            npr.append(p)
        return tuple(npr)

    lax.fori_loop(0, seg, fixup, ones)
    if rev:
        for n in range(LRU_BLOCKS):
            o_ref[0, :, n * LRU_BLOCK:(n + 1) * LRU_BLOCK] = on_ref[n].astype(o_ref.dtype)


def _lru_pass(cols1, cw, cb, w, ba, bx, lam, layer, rev, tt, pad, h_fwd=None):
    bsz, lq, _ = cols1.shape
    nt = lq // tt
    d = 1 if rev else 0
    nh = lq // HALO
    th = tt // HALO
    tmap = (lambda i: nt - 1 - i) if rev else (lambda i: i)

    def slabs(col0):
        return [pl.BlockSpec((1, tt, LRU_BLOCK), functools.partial(lambda b, i, n: (b, tmap(i), n), n=col0 + n))
                for n in range(LRU_BLOCKS)]

    in_specs = slabs(C1_X // LRU_BLOCK) + [
        pl.BlockSpec((1, HALO, LRU_WIDTH), lambda b, i: (b, jnp.maximum(tmap(i) * th - 1, 0), 0)),
        pl.BlockSpec((1, HALO, LRU_WIDTH), lambda b, i: (b, jnp.minimum((tmap(i) + 1) * th, nh - 1), 0)),
        pl.BlockSpec((1, 8, LRU_WIDTH), lambda b, i: (layer, 0, 0)),
        pl.BlockSpec((1, 1, LRU_WIDTH), lambda b, i: (layer, 0, 0)),
        pl.BlockSpec((1, LRU_BLOCKS, LRU_BLOCK, 2 * LRU_BLOCK), lambda b, i: (layer * 2 + d, 0, 0, 0)),
        pl.BlockSpec((1, 1, LRU_WIDTH), lambda b, i: (layer * 2 + d, 0, 0)),
        pl.BlockSpec((1, 1, LRU_WIDTH), lambda b, i: (layer * 2 + d, 0, 0)),
        pl.BlockSpec((1, 1, LRU_WIDTH), lambda b, i: (layer * 2 + d, 0, 0)),
    ]
    args = [cols1] * (LRU_BLOCKS + 2) + [cw, cb, w, ba, bx, lam]
    scratch = [
        pltpu.VMEM((LRU_BLOCKS, tt, LRU_BLOCK), F32),
        pltpu.VMEM((LRU_BLOCKS, tt, LRU_BLOCK), F32),
        pltpu.VMEM((LRU_BLOCKS, 1, LRU_BLOCK), F32),
    ]
    if rev:
        in_specs += [pl.BlockSpec((LRU_BLOCKS, 1, tt, LRU_BLOCK), lambda b, i: (0, b, tmap(i), 0))]
        in_specs += slabs(C1_GC // LRU_BLOCK)
        args += [h_fwd] + [cols1] * LRU_BLOCKS
        scratch += [pltpu.VMEM((LRU_BLOCKS, tt, LRU_BLOCK), F32)]
        out_spec = pl.BlockSpec((1, tt, LRU_WIDTH), lambda b, i: (b, tmap(i), 0))
        out_shape = jax.ShapeDtypeStruct((bsz, lq, LRU_WIDTH), BF16)
    else:
        out_spec = pl.BlockSpec((LRU_BLOCKS, 1, tt, LRU_BLOCK), lambda b, i: (0, b, tmap(i), 0))
        out_shape = jax.ShapeDtypeStruct((LRU_BLOCKS, bsz, lq, LRU_BLOCK), F32)
    return pl.pallas_call(
        functools.partial(_lru_kernel, rev=rev, tt=tt, nt=nt, pad=pad),
        grid=(bsz, nt),
        in_specs=in_specs,
        out_specs=out_spec,
        out_shape=out_shape,
        scratch_shapes=scratch,
        compiler_params=_params(("parallel", "arbitrary")),
        name="lru_bwd" if rev else "lru_fwd",
    )(*args)


def _out_kernel(z_ref, g_ref, mask_ref, ys_ref, ga_ref, yb_ref, yc_ref, wglu_ref, bglu_ref,
                wm0_ref, wm1_ref, wm2_ref, woa_ref, wob_ref, woc_ref, wo_ref, fg_ref,
                o_ref, h_ref, ya_ref, *, final):
    j = pl.program_id(1)
    nj = pl.num_programs(1)

    @pl.when(j == 0)
    def _():
        z = z_ref[...]
        ms = jnp.mean(z * z, axis=-1, keepdims=True)
        h_ref[...] = (z * lax.rsqrt(ms + EPS) * g_ref[0]).astype(BF16)
        zz = jax.nn.gelu(ys_ref[...])
        glu = _sigmoid(_dot(zz.astype(BF16), wglu_ref[0]) + bglu_ref[0])
        ga = ga_ref[...]
        ya_ref[...] = (zz * glu * (ga * _sigmoid(ga))).astype(BF16)

    h = h_ref[...]
    m = (_sigmoid(_dot(h, wm0_ref[0])) * _dot(ya_ref[...], woa_ref[0])
         + _sigmoid(_dot(h, wm1_ref[0])) * _dot(yb_ref[...], wob_ref[0])
         + _sigmoid(_dot(h, wm2_ref[0])) * _dot(yc_ref[...], woc_ref[0]))
    contrib = _dot(m.astype(BF16), wo_ref[0])

    @pl.when(j == 0)
    def _():
        o_ref[...] = contrib

    @pl.when(jnp.logical_and(j > 0, j < nj - 1))
    def _():
        o_ref[...] += contrib

    @pl.when(j == nj - 1)
    def _():
        zn = z_ref[...] + (o_ref[...] + contrib)
        if final:
            ms = jnp.mean(zn * zn, axis=-1, keepdims=True)
            o_ref[...] = zn * lax.rsqrt(ms + EPS) * fg_ref[...]
        else:
            o_ref[...] = jnp.where(mask_ref[...] > 0.0, zn, 0.0)


def _out_proj(z, g, mask, cols1, ys, yb, yc, wglu, bglu, wm, woa, wob, woc, wo, fg, layer, tm, final):
    r, d = z.shape
    TJ = TJ_WIDE if tm <= OUT_TILE else TJ_NARROW
    nj = d // TJ
    return pl.pallas_call(
        functools.partial(_out_kernel, final=final),
        grid=(r // tm, nj),
        in_specs=[
            pl.BlockSpec((tm, d), lambda i, j: (i, 0)),
            pl.BlockSpec((1, 1, d), lambda i, j: (layer, 0, 0)),
            pl.BlockSpec((tm, 1), lambda i, j: (i, 0)),
            pl.BlockSpec((tm, S5_WIDTH), lambda i, j: (i, 0)),
            pl.BlockSpec((tm, S5_WIDTH), lambda i, j: (i, C1_GA // S5_WIDTH)),
            pl.BlockSpec((tm, GLA_WIDTH), lambda i, j: (i, 0)),
            pl.BlockSpec((tm, LRU_WIDTH), lambda i, j: (i, 0)),
            pl.BlockSpec((1, S5_WIDTH, S5_WIDTH), lambda i, j: (layer, 0, 0)),
            pl.BlockSpec((1, 1, S5_WIDTH), lambda i, j: (layer, 0, 0)),
            pl.BlockSpec((1, d, TJ), lambda i, j: (layer, 0, j)),
            pl.BlockSpec((1, d, TJ), lambda i, j: (layer, 0, nj + j)),
            pl.BlockSpec((1, d, TJ), lambda i, j: (layer, 0, 2 * nj + j)),
            pl.BlockSpec((1, S5_WIDTH, TJ), lambda i, j: (layer, 0, j)),
            pl.BlockSpec((1, GLA_WIDTH, TJ), lambda i, j: (layer, 0, j)),
            pl.BlockSpec((1, LRU_WIDTH, TJ), lambda i, j: (layer, 0, j)),
            pl.BlockSpec((1, TJ, d), lambda i, j: (layer, j, 0)),
            pl.BlockSpec((1, d), lambda i, j: (0, 0)),
        ],
        out_specs=pl.BlockSpec((tm, d), lambda i, j: (i, 0)),
        out_shape=jax.ShapeDtypeStruct((r, d), F32),
        scratch_shapes=[
            pltpu.VMEM((tm, d), BF16),
            pltpu.VMEM((tm, S5_WIDTH), BF16),
        ],
        compiler_params=_params(("parallel", "arbitrary")),
        name="out_proj",
    )(z, g, mask, ys, cols1, yb, yc, wglu, bglu, wm, wm, wm, woa, wob, woc, wo, fg)


def _prepare(p):
    depth = p["w_in"].shape[0]
    w_in = p["w_in"]
    o = 0
    seg = {}
    for name, width in (("u", 512), ("ga", 512), ("q", 256), ("k", 256), ("v", 512), ("gb", 512),
                        ("glr", 2 * GLA_RANK), ("x", 1024), ("gc", 1024), ("m", 3 * w_in.shape[1])):
        seg[name] = w_in[:, :, o:o + width]
        o += width
    w1 = jnp.concatenate([seg["x"], seg["gc"], seg["u"], seg["ga"]], axis=-1).astype(BF16)
    zpad = jnp.zeros(w_in.shape[:2] + (N2 - C2_GLR - 2 * GLA_RANK,), w_in.dtype)
    w2 = jnp.concatenate([seg["v"], seg["gb"], seg["q"], seg["k"], seg["glr"], zpad], axis=-1).astype(BF16)
    wm = seg["m"].astype(BF16)

    ng = S5_WIDTH // 128
    eye = jnp.eye(S5_GB, dtype=F32)

    def rows(a):
        return a.astype(F32).reshape(depth, 2, ng, S5_NS)

    ls = jnp.broadcast_to(p["s5_log_step"].astype(F32)[..., None], p["s5_lam_re"].shape)
    lr, li, ls = rows(p["s5_lam_re"]), rows(p["s5_lam_im"]), rows(ls)
    zero = jnp.zeros_like(lr[:, 0])
    prm = jnp.stack([lr[:, 0], li[:, 0], ls[:, 0], lr[:, 1], li[:, 1], ls[:, 1], zero, zero], axis=2)

    def emb_b(bm):
        bm = bm.astype(F32).reshape(depth, ng, S5_GB, 64, 16)
        return jnp.einsum("lGgnc,gh->lGgchn", bm, eye).reshape(depth, ng, 128, S5_NS)

    def emb_c(cm):
        cm = cm.astype(F32).reshape(depth, 2, ng, S5_GB, 16, 64)
        return jnp.einsum("ldGgcn,gh->ldGgchn", cm, eye).reshape(depth, 2, ng, 128, S5_NS)

    wg = jnp.zeros((depth, 2, 128, GLA_KEY), F32)
    for d in range(2):
        wg = wg.at[:, d, d * GLA_RANK:(d + 1) * GLA_RANK, :].set(p["gla_w_gate_up"][:, d].astype(F32))
    lru_w = (0.5 * jnp.concatenate([p["lru_w_a"], p["lru_w_x"]], axis=-1)).astype(BF16)
    cw = jnp.concatenate([p["conv_w"].astype(F32),
                          jnp.zeros((depth, 8 - CONV_WIDTH, LRU_WIDTH), F32)], axis=1)
    return dict(
        norm_g=p["norm_g"].astype(F32).reshape(depth, 1, -1), w1=w1, w2=w2, wm=wm,
        s5_prm=prm, s5_lbr=emb_b(p["s5_b_re"]), s5_lbi=emb_b(p["s5_b_im"]),
        s5_ctr=emb_c(p["s5_c_re"]), s5_cti=emb_c(p["s5_c_im"]),
        s5_d=p["s5_d"].astype(F32).reshape(depth, ng, 1, 128),
        wglu=p["s5_w_glu"].astype(BF16), bglu=p["s5_b_glu"].astype(F32).reshape(depth, 1, S5_WIDTH),
        wg=wg.reshape(depth * 2, 128, GLA_KEY).astype(BF16),
        bg=p["gla_b_gate"].astype(F32).reshape(depth * 2, 1, GLA_KEY),
        gla_ng=p["gla_norm_g"].astype(F32).reshape(depth, 1, GLA_WIDTH),
        cw=cw, cb=p["conv_b"].astype(F32).reshape(depth, 1, LRU_WIDTH),
        lru_w=lru_w.reshape(depth * 2, LRU_BLOCKS, LRU_BLOCK, 2 * LRU_BLOCK),
        lru_ba=p["lru_b_a"].astype(F32).reshape(depth * 2, 1, LRU_WIDTH),
        lru_bx=p["lru_b_x"].astype(F32).reshape(depth * 2, 1, LRU_WIDTH),
        lru_lam=p["lru_lam"].astype(F32).reshape(depth * 2, 1, LRU_WIDTH),
        woa=p["w_out_a"].astype(BF16), wob=p["w_out_b"].astype(BF16), woc=p["w_out_c"].astype(BF16),
        wo=p["w_o"].astype(BF16), fg=p["final_norm_g"].astype(F32).reshape(1, -1),
    )


def _pick_tile(n, unit, target):
    best = unit
    for t in range(unit, target + 1, unit):
        if n % t == 0:
            best = t
    return best


def _pick_lru_tile(lq, target):
    best = None
    for seg in range(1, target // 8 + 1):
        if lq % (8 * seg) == 0 and seg % 8 != 0:
            best = 8 * seg
    assert best is not None
    return best


def _encoder(x, meta, w):
    bsz, seq, d = x.shape
    depth = w["w2"].shape[0]
    pad = PAD
    while (bsz * (pad + N_META + seq)) % OUT_TILE != 0 and pad < PAD + 4 * GLA_CHUNK:
        pad += GLA_CHUNK
    if (bsz * (pad + N_META + seq)) % OUT_TILE != 0:
        pad = PAD
    lq = pad + N_META + seq
    r = bsz * lq
    head = jnp.concatenate([jnp.zeros((pad, d), F32), meta.astype(F32)], axis=0)
    z = jnp.concatenate([jnp.broadcast_to(head[None], (bsz, pad + N_META, d)), x.astype(F32)], axis=1)
    z = z.reshape(r, d)
    mask = jnp.broadcast_to((jnp.arange(lq) >= pad).astype(F32)[None, :, None], (bsz, lq, 1)).reshape(r, 1)
    tm = _pick_tile(r, LANES, IN_TILE_MAX)
    tm_out = _pick_tile(r, LANES, OUT_TILE_MAX)
    cpt = _pick_tile(lq // GLA_CHUNK, 1, GLA_CHUNKS_MAX)
    tt = _pick_lru_tile(lq, LRU_TILE_MAX)
    for l in range(depth):
        cols1, cols2 = _in_proj(z, w["norm_g"], w["w1"], w["w2"], l, tm)
        c1 = cols1.reshape(bsz, lq, N1)
        c2 = cols2.reshape(bsz, lq, N2)
        ys = _s5(c1, w["s5_prm"], w["s5_lbr"], w["s5_lbi"], w["s5_ctr"], w["s5_cti"], w["s5_d"], l)
        of = _gla_pass(c2, w["wg"], w["bg"], l, False, cpt, pad)
        yb = _gla_pass(c2, w["wg"], w["bg"], l, True, cpt, pad, of, w["gla_ng"])
        lru_w = (w["cw"], w["cb"], w["lru_w"], w["lru_ba"], w["lru_bx"], w["lru_lam"])
        hf = _lru_pass(c1, *lru_w, l, False, tt, pad)
        yc = _lru_pass(c1, *lru_w, l, True, tt, pad, hf)
        z = _out_proj(z, w["norm_g"], mask, cols1, ys.reshape(r, -1), yb.reshape(r, -1),
                      yc.reshape(r, -1), w["wglu"], w["bglu"], w["wm"], w["woa"], w["wob"], w["woc"],
                      w["wo"], w["fg"], l, tm_out, l == depth - 1)
    return z.reshape(bsz, lq, d)[:, pad + N_META:].astype(x.dtype)


def kernel(x_prompt, x_sample, meta_tokens, norm_g, w_in, s5_lam_re, s5_lam_im, s5_log_step, s5_b_re, s5_b_im, s5_c_re, s5_c_im, s5_d, s5_w_glu, s5_b_glu, gla_w_gate_up, gla_b_gate, gla_norm_g, conv_w, conv_b, lru_w_a, lru_b_a, lru_w_x, lru_b_x, lru_lam, w_out_a, w_out_b, w_out_c, w_o, final_norm_g):
    w = _prepare(dict(
        norm_g=norm_g, w_in=w_in, s5_lam_re=s5_lam_re, s5_lam_im=s5_lam_im, s5_log_step=s5_log_step,
        s5_b_re=s5_b_re, s5_b_im=s5_b_im, s5_c_re=s5_c_re, s5_c_im=s5_c_im, s5_d=s5_d,
        s5_w_glu=s5_w_glu, s5_b_glu=s5_b_glu, gla_w_gate_up=gla_w_gate_up, gla_b_gate=gla_b_gate,
        gla_norm_g=gla_norm_g, conv_w=conv_w, conv_b=conv_b, lru_w_a=lru_w_a, lru_b_a=lru_b_a,
        lru_w_x=lru_w_x, lru_b_x=lru_b_x, lru_lam=lru_lam, w_out_a=w_out_a, w_out_b=w_out_b,
        w_out_c=w_out_c, w_o=w_o, final_norm_g=final_norm_g))
    return (_encoder(x_prompt, meta_tokens, w), _encoder(x_sample, meta_tokens, w))
```

```python
import functools

import jax
import jax.numpy as jnp
from jax import lax
from jax.experimental import pallas as pl
from jax.experimental.pallas import tpu as pltpu

F32 = jnp.float32
BF16 = jnp.bfloat16
HIGHEST = lax.Precision.HIGHEST

N_META = 16
PAD = 48
EPS = 1e-6
S5_WIDTH = 512
S5_T = 8
S5_GB = 8
S5_NS = 512
GLA_HEADS = 4
GLA_DK = 64
GLA_DV = 128
GLA_KEY = 256
GLA_WIDTH = 512
GLA_RANK = 16
GLA_GATE_NORM = 16.0
GLA_CHUNK = 64
LRU_WIDTH = 1024
LRU_BLOCKS = 8
LRU_BLOCK = 128
CONV_WIDTH = 4
LRU_C = 8.0
HALO = 8
LANES = 128
VMEM_LIMIT = 56 * 1024 * 1024
IN_TILE_MAX = 768
OUT_TILE_MAX = 640
OUT_TILE = 512
GLA_CHUNKS_MAX = 17
LRU_TILE_MAX = 640

C1_X, C1_GC, C1_U, C1_GA = 0, 1024, 2048, 2560
N1 = 3072
C2_V, C2_GB, C2_Q, C2_K, C2_GLR = 0, 512, 1024, 1280, 1536
N2 = 1664
TN = 1024
NJ1 = N1 // TN
TJ_WIDE, TJ_NARROW = 512, 256


def _dot(a, b):
    return jnp.dot(a, b, preferred_element_type=F32)


def _dot_nt(a, b):
    return lax.dot_general(a, b, (((1,), (1,)), ((), ())), preferred_element_type=F32)


def _dot_tn(a, b):
    return lax.dot_general(a, b, (((0,), (0,)), ((), ())), preferred_element_type=F32)


def _sigmoid(x):
    return 0.5 * jnp.tanh(0.5 * x) + 0.5


def _params(sem):
    return pltpu.CompilerParams(dimension_semantics=sem, vmem_limit_bytes=VMEM_LIMIT)


def _kin_kernel(z_ref, g_ref, w1_ref, w2_ref, o1_ref, o2_ref, h_ref):
    j = pl.program_id(1)

    @pl.when(j == 0)
    def _():
        z = z_ref[...]
        ms = jnp.mean(z * z, axis=-1, keepdims=True)
        h_ref[...] = (z * lax.rsqrt(ms + EPS) * g_ref[0]).astype(BF16)

    @pl.when(j < NJ1)
    def _():
        o1_ref[...] = _dot(h_ref[...], w1_ref[0])

    @pl.when(j == NJ1)
    def _():
        o2_ref[...] = _dot(h_ref[...], w2_ref[0]).astype(o2_ref.dtype)


def _in_proj(z, g, w1, w2, layer, tm):
    r, d = z.shape
    last = NJ1 - 1
    return pl.pallas_call(
        _kin_kernel,
        grid=(r // tm, NJ1 + 1),
        in_specs=[
            pl.BlockSpec((tm, d), lambda i, j: (i, 0)),
            pl.BlockSpec((1, 1, d), lambda i, j: (layer, 0, 0)),
            pl.BlockSpec((1, d, TN), lambda i, j: (layer, 0, jnp.minimum(j, last))),
            pl.BlockSpec((1, d, N2), lambda i, j: (layer, 0, 0)),
        ],
        out_specs=[
            pl.BlockSpec((tm, TN), lambda i, j: (i, jnp.minimum(j, last))),
            pl.BlockSpec((tm, N2), lambda i, j: (i, 0)),
        ],
        out_shape=[jax.ShapeDtypeStruct((r, N1), F32), jax.ShapeDtypeStruct((r, N2), BF16)],
        scratch_shapes=[pltpu.VMEM((tm, d), BF16)],
        compiler_params=_params(("parallel", "arbitrary")),
        name="in_proj",
    )(z, g, w1, w2)


def _s5_kernel(u_ref, prm_ref, lbr_ref, lbi_ref, ctr_ref, cti_ref, dsk_ref, o_ref,
               wst_ref, m_ref, wout_ref, e_ref, w_ref, s_ref, apow_ref, x_ref, *, nc, bsub):
    b = pl.program_id(1)
    ns = S5_NS
    blk = S5_GB * 16
    ncp = -(-nc // 8) * 8

    @pl.when(b == 0)
    def _build():
        prm = prm_ref[0, 0]
        evec = lax.broadcasted_iota(jnp.int32, (24, 1), 0).astype(F32)
        e_ref[...] = jnp.zeros_like(e_ref)
        for d in range(2):
            lr = prm[3 * d:3 * d + 1]
            li = prm[3 * d + 1:3 * d + 2]
            dt = jnp.exp(prm[3 * d + 2:3 * d + 3])
            mag = jnp.exp(evec * (lr * dt))
            ang = evec * (li * dt)
            pre = mag * jnp.cos(ang)
            pim = mag * jnp.sin(ang)
            abr = pre[1:2]
            abi = pim[1:2]
            den = lr * lr + li * li
            fr = ((abr - 1.0) * lr + abi * li) / den
            fi = (abi * lr - (abr - 1.0) * li) / den
            lbr = lbr_ref[0, 0]
            lbi = lbi_ref[0, 0]
            bbr = fr * lbr - fi * lbi
            bbi = fr * lbi + fi * lbr
            ctr = ctr_ref[0, d, 0]
            cti = cti_ref[0, d, 0]
            apow_ref[2 * d:2 * d + 1, :] = pre[S5_T:S5_T + 1]
            apow_ref[2 * d + 1:2 * d + 2, :] = pim[S5_T:S5_T + 1]
            sr, si = jnp.ones_like(lr), jnp.zeros_like(lr)
            br_, bi_ = pre[S5_T:S5_T + 1], pim[S5_T:S5_T + 1]
            nseg = ncp // 8
            while nseg:
                if nseg & 1:
                    sr, si = sr * br_ - si * bi_, sr * bi_ + si * br_
                br_, bi_ = br_ * br_ - bi_ * bi_, 2.0 * br_ * bi_
                nseg >>= 1
            apow_ref[4 + 2 * d:5 + 2 * d, :] = sr
            apow_ref[5 + 2 * d:6 + 2 * d, :] = si
            for r in range(S5_T):
                e = S5_T - 1 - r if d == 0 else r
                pr_e = pre[e:e + 1]
                pi_e = pim[e:e + 1]
                l_re = bbr * pr_e - bbi * pi_e
                l_im = bbr * pi_e + bbi * pr_e
                base = 2 * d * ns
                wst_ref[r * blk:(r + 1) * blk, base:base + ns] = l_re.astype(BF16)
                wst_ref[r * blk:(r + 1) * blk, base + ns:base + 2 * ns] = l_im.astype(BF16)
                kblk = _dot_nt(l_re.astype(BF16), ctr.astype(BF16)) - _dot_nt(l_im.astype(BF16), cti.astype(BF16))
                eb = r if d == 0 else S5_T - 1 + r
                e_ref[eb * blk:(eb + 1) * blk, :] += kblk
                eo = r + 1 if d == 0 else S5_T - r
                pr_o = pre[eo:eo + 1]
                pi_o = pim[eo:eo + 1]
                wout_ref[r * blk:(r + 1) * blk, base:base + ns] = (pr_o * ctr - pi_o * cti).astype(BF16)
                wout_ref[r * blk:(r + 1) * blk, base + ns:base + 2 * ns] = (
                    -(pi_o * ctr) - pr_o * cti).astype(BF16)
        for t in range(S5_T):
            off = (S5_T - 1 - t) * blk
            m_ref[:, t * blk:(t + 1) * blk] = e_ref[off:off + S5_T * blk, :].astype(BF16)

    for bb in range(bsub):
        for t in range(S5_T):
            x_ref[bb * ncp:bb * ncp + nc, t * blk:(t + 1) * blk] = u_ref[bb, pl.ds(t, nc, stride=S5_T), :]
        if ncp > nc:
            x_ref[bb * ncp + nc:(bb + 1) * ncp, :] = jnp.zeros((ncp - nc, S5_T * blk), F32)
    x = x_ref[...]
    xb = x.astype(BF16)
    wv = _dot(xb, wst_ref[...])
    nq = ns // 128
    for kk in range(4 * nq):
        w_ref[kk] = wv[:, kk * 128:(kk + 1) * 128]

    seg = ncp // 8
    apow = apow_ref[...]
    pairs = [(d, q) for d in range(2) for q in range(nq)]

    def coef(row, q):
        return apow[row:row + 1, q * 128:(q + 1) * 128]

    zero8 = jnp.zeros((8, 128), F32)
    one8 = jnp.ones((8, 128), F32)

    def scan_sequence(base):
        def rows(d, q, j):
            jj = base + (j if d == 0 else seg - 1 - j)
            return ((2 * d * nq + q, pl.ds(jj, 8, stride=seg), slice(None)),
                    ((2 * d + 1) * nq + q, pl.ds(jj, 8, stride=seg), slice(None)))

        def local(j, carry):
            new = []
            for n, (d, q) in enumerate(pairs):
                cr, ci = carry[2 * n], carry[2 * n + 1]
                ir, ii = rows(d, q, j)
                ar, ai = coef(2 * d, q), coef(2 * d + 1, q)
                s_ref[ir] = cr
                s_ref[ii] = ci
                new += [ar * cr - ai * ci + w_ref[ir], ar * ci + ai * cr + w_ref[ii]]
            return tuple(new)

        totals = lax.fori_loop(0, seg, local, tuple(zero8 for _ in range(2 * len(pairs))))

        cins = []
        for n, (d, q) in enumerate(pairs):
            tr, ti = totals[2 * n], totals[2 * n + 1]
            sr, si = coef(4 + 2 * d, q), coef(5 + 2 * d, q)
            cr = jnp.zeros((1, 128), F32)
            ci = jnp.zeros((1, 128), F32)
            rr, ri = [None] * 8, [None] * 8
            for s in (range(8) if d == 0 else range(7, -1, -1)):
                rr[s], ri[s] = cr, ci
                cr, ci = sr * cr - si * ci + tr[s:s + 1], sr * ci + si * cr + ti[s:s + 1]
            cins += [jnp.concatenate(rr, axis=0), jnp.concatenate(ri, axis=0)]

        def fixup(j, ps):
            new = []
            for n, (d, q) in enumerate(pairs):
                pr, pi = ps[2 * n], ps[2 * n + 1]
                ir, ii = rows(d, q, j)
                cr, ci = cins[2 * n], cins[2 * n + 1]
                s_ref[ir] = s_ref[ir] + (pr * cr - pi * ci)
                s_ref[ii] = s_ref[ii] + (pr * ci + pi * cr)
                ar, ai = coef(2 * d, q), coef(2 * d + 1, q)
                new += [ar * pr - ai * pi, ar * pi + ai * pr]
            return tuple(new)

        lax.fori_loop(0, seg, fixup, tuple(one8 if n % 2 == 0 else zero8 for n in range(2 * len(pairs))))

    for bb in range(bsub):
        scan_sequence(bb * ncp)

    s_in = jnp.concatenate([s_ref[kk] for kk in range(4 * nq)], axis=1).astype(BF16)
    y = (_dot(xb, m_ref[...]) + _dot_nt(s_in, wout_ref[...])
         + x * jnp.concatenate([dsk_ref[0, 0]] * S5_T, axis=1))
    for bb in range(bsub):
        for t in range(S5_T):
            o_ref[bb, pl.ds(t, nc, stride=S5_T), :] = y[bb * ncp:bb * ncp + nc, t * blk:(t + 1) * blk]


def _s5(cols1, prm, lbr, lbi, ctr, cti, dsk, layer):
    bsz, lq, _ = cols1.shape
    nc = lq // S5_T
    ncp = -(-nc // 8) * 8
    ng = S5_WIDTH // LANES
    n = S5_T * LANES
    ublk = C1_U // LANES
    bsub = 2 if (bsz % 2 == 0 and 2 * ncp <= 640) else 1
    return pl.pallas_call(
        functools.partial(_s5_kernel, nc=nc, bsub=bsub),
        grid=(ng, bsz // bsub),
        in_specs=[
            pl.BlockSpec((bsub, lq, 128), lambda g, b: (b, 0, ublk + g)),
            pl.BlockSpec((1, 1, 8, S5_NS), lambda g, b: (layer, g, 0, 0)),
            pl.BlockSpec((1, 1, 128, S5_NS), lambda g, b: (layer, g, 0, 0)),
            pl.BlockSpec((1, 1, 128, S5_NS), lambda g, b: (layer, g, 0, 0)),
            pl.BlockSpec((1, 2, 1, 128, S5_NS), lambda g, b: (layer, 0, g, 0, 0)),
            pl.BlockSpec((1, 2, 1, 128, S5_NS), lambda g, b: (layer, 0, g, 0, 0)),
            pl.BlockSpec((1, 1, 1, 128), lambda g, b: (layer, g, 0, 0)),
        ],
        out_specs=pl.BlockSpec((bsub, lq, 128), lambda g, b: (b, 0, g)),
        out_shape=jax.ShapeDtypeStruct((bsz, lq, S5_WIDTH), F32),
        scratch_shapes=[
            pltpu.VMEM((n, 4 * S5_NS), BF16),
            pltpu.VMEM((n, n), BF16),
            pltpu.VMEM((n, 4 * S5_NS), BF16),
            pltpu.VMEM(((2 * S5_T - 1) * 128, 128), F32),
            pltpu.VMEM((4 * S5_NS // 128, bsub * ncp, 128), F32),
            pltpu.VMEM((4 * S5_NS // 128, bsub * ncp, 128), F32),
            pltpu.VMEM((8, S5_NS), F32),
            pltpu.VMEM((bsub * ncp, n), F32),
        ],
        compiler_params=_params(("arbitrary", "arbitrary")),
        name="s5_mixer",
    )(cols1, prm, lbr, lbi, ctr, cti, dsk)


def _gla_kernel(*refs, rev, cpt, nt, pad):
    if rev:
        (q_ref, k_ref, v_ref, glr_ref, wg_ref, bg_ref, of_ref, gate_ref, ng_ref, o_ref, st_ref) = refs
    else:
        (q_ref, k_ref, v_ref, glr_ref, wg_ref, bg_ref, o_ref, st_ref) = refs
    i = pl.program_id(1)
    tile = nt - 1 - i if rev else i
    ch = GLA_CHUNK

    @pl.when(i == 0)
    def _():
        st_ref[...] = jnp.zeros_like(st_ref)

    tt = cpt * ch
    row = lax.broadcasted_iota(jnp.int32, (ch, ch), 0)
    col = lax.broadcasted_iota(jnp.int32, (ch, ch), 1)
    causal = (col >= row) if rev else (col <= row)
    tri = jnp.where(causal, 1.0, 0.0).astype(F32)

    glin = _dot(glr_ref[0].astype(BF16), wg_ref[0]) + bg_ref[0]
    g = jax.nn.log_sigmoid(glin) / GLA_GATE_NORM
    t_glob = tile * tt + lax.broadcasted_iota(jnp.int32, (tt, 1), 0)
    g = jnp.where(t_glob >= pad, g, 0.0)
    bcs = [jnp.dot(tri, g[c * ch:(c + 1) * ch], precision=HIGHEST, preferred_element_type=F32)
           for c in range(cpt)]
    blasts = [bc[0:1] if rev else bc[ch - 1:ch] for bc in bcs]
    bcum = jnp.concatenate(bcs, axis=0)
    brel = jnp.concatenate([blasts[c] - bcs[c] for c in range(cpt)], axis=0)
    k = k_ref[0].astype(F32)
    qe = (q_ref[0].astype(F32) * (GLA_DK ** -0.5) * jnp.exp(bcum)).astype(BF16)
    ke = (k * jnp.exp(-bcum)).astype(BF16)
    kd = (k * jnp.exp(brel)).astype(BF16)
    v = v_ref[0].astype(BF16)

    npair = GLA_HEADS // 2
    kw, vw = 2 * GLA_DK, 2 * GLA_DV
    hi = lambda shape, axis, shift: lax.broadcasted_iota(jnp.int32, shape, axis) >> shift
    bm_k = hi((2 * ch, kw), 0, 6) == hi((2 * ch, kw), 1, 6)
    bm_v = hi((2 * ch, vw), 0, 6) == hi((2 * ch, vw), 1, 7)
    bm_s = hi((vw, kw), 0, 7) == hi((vw, kw), 1, 6)
    arow = lax.broadcasted_iota(jnp.int32, (ch, 2 * ch), 0)
    acol = lax.broadcasted_iota(jnp.int32, (ch, 2 * ch), 1) & (ch - 1)
    causal2 = (acol >= arow) if rev else (acol <= arow)
    zk = jnp.zeros((), BF16)
    states = [st_ref[p] for p in range(npair)]
    for ci in range(cpt):
        c = cpt - 1 - ci if rev else ci
        rs = slice(c * ch, (c + 1) * ch)
        dec = jnp.exp(blasts[c])
        outs = []
        for p in range(npair):
            ks = slice(p * kw, (p + 1) * kw)
            vs = slice(p * vw, (p + 1) * vw)
            qp = qe[rs, ks]
            ke2 = jnp.where(bm_k, jnp.concatenate([ke[rs, ks]] * 2, axis=0), zk)
            v2 = jnp.where(bm_v, jnp.concatenate([v[rs, vs]] * 2, axis=0), zk)
            att = jnp.where(causal2, _dot_nt(qp, ke2), 0.0)
            outs.append(_dot(att.astype(BF16), v2) + _dot_nt(qp, states[p].astype(BF16)))
            states[p] = states[p] * dec[:, ks] + jnp.where(bm_s, _dot_tn(v[rs, vs], kd[rs, ks]), 0.0)
        o = jnp.concatenate(outs, axis=1)
        if rev:
            o = o + of_ref[0, rs, :]
            parts = []
            for h in range(GLA_HEADS):
                oh = o[:, h * GLA_DV:(h + 1) * GLA_DV]
                parts.append(oh * lax.rsqrt(jnp.mean(oh * oh, axis=-1, keepdims=True) + EPS))
            o = jnp.concatenate(parts, axis=1) * ng_ref[0]
            gate = gate_ref[0, rs, :].astype(F32)
            o = o * (gate * _sigmoid(gate))
        o_ref[0, rs, :] = o.astype(o_ref.dtype)
    for p in range(npair):
        st_ref[p] = states[p]


def _gla_pass(cols2, wg, bg, layer, rev, cpt, pad, o_fwd=None, norm_g=None):
    bsz, lq, _ = cols2.shape
    tt = cpt * GLA_CHUNK
    nt = lq // tt
    d = 1 if rev else 0
    tmap = (lambda i: nt - 1 - i) if rev else (lambda i: i)
    in_specs = [
        pl.BlockSpec((1, tt, GLA_KEY), lambda b, i: (b, tmap(i), C2_Q // GLA_KEY)),
        pl.BlockSpec((1, tt, GLA_KEY), lambda b, i: (b, tmap(i), C2_K // GLA_KEY)),
        pl.BlockSpec((1, tt, GLA_WIDTH), lambda b, i: (b, tmap(i), C2_V // GLA_WIDTH)),
        pl.BlockSpec((1, tt, 128), lambda b, i: (b, tmap(i), C2_GLR // 128)),
        pl.BlockSpec((1, 128, GLA_KEY), lambda b, i: (layer * 2 + d, 0, 0)),
        pl.BlockSpec((1, 1, GLA_KEY), lambda b, i: (layer * 2 + d, 0, 0)),
    ]
    args = [cols2, cols2, cols2, cols2, wg, bg]
    if rev:
        in_specs += [
            pl.BlockSpec((1, tt, GLA_WIDTH), lambda b, i: (b, tmap(i), 0)),
            pl.BlockSpec((1, tt, GLA_WIDTH), lambda b, i: (b, tmap(i), C2_GB // GLA_WIDTH)),
            pl.BlockSpec((1, 1, GLA_WIDTH), lambda b, i: (layer, 0, 0)),
        ]
        args += [o_fwd, cols2, norm_g]
    return pl.pallas_call(
        functools.partial(_gla_kernel, rev=rev, cpt=cpt, nt=nt, pad=pad),
        grid=(bsz, nt),
        in_specs=in_specs,
        out_specs=pl.BlockSpec((1, tt, GLA_WIDTH), lambda b, i: (b, tmap(i), 0)),
        out_shape=jax.ShapeDtypeStruct((bsz, lq, GLA_WIDTH), BF16 if rev else F32),
        scratch_shapes=[pltpu.VMEM((GLA_HEADS // 2, 2 * GLA_DV, 2 * GLA_DK), F32)],
        compiler_params=_params(("parallel", "arbitrary")),
        name="gla_bwd" if rev else "gla_fwd",
    )(*args)


def _lru_kernel(*refs, rev, tt, nt, pad):
    xs = refs[:LRU_BLOCKS]
    refs = refs[LRU_BLOCKS:]
    if rev:
        (xp_ref, xn_ref, cw_ref, cb_ref, w_ref, ba_ref, bx_ref, lam_ref, hf_ref) = refs[:9]
        gates = refs[9:9 + LRU_BLOCKS]
        o_ref, a_ref, b_ref, h_ref, on_ref = refs[9 + LRU_BLOCKS:]
    else:
        (xp_ref, xn_ref, cw_ref, cb_ref, w_ref, ba_ref, bx_ref, lam_ref,
         o_ref, a_ref, b_ref, h_ref) = refs
    i = pl.program_id(1)
    tile = nt - 1 - i if rev else i
    seg = tt // 8

    @pl.when(i == 0)
    def _():
        h_ref[...] = jnp.zeros_like(h_ref)

    def permuted(ref, lead):
        return jnp.concatenate([ref[lead + (pl.ds(j, 8, stride=seg), slice(None))] for j in range(seg)], axis=0)

    prev = jnp.where(tile == 0, 0.0, xp_ref[0])
    nxt = jnp.where(tile == nt - 1, 0.0, xn_ref[0])
    sub = lax.broadcasted_iota(jnp.int32, (8, LRU_BLOCK), 0)
    left = CONV_WIDTH // 2
    cw = cw_ref[0]
    sp = jax.nn.softplus(-lam_ref[0])
    rperm = lax.broadcasted_iota(jnp.int32, (tt, 1), 0)
    valid = (tile * tt + (rperm & 7) * seg + (rperm >> 3)) >= pad
    for n in range(LRU_BLOCKS):
        sl = slice(n * LRU_BLOCK, (n + 1) * LRU_BLOCK)
        xp = permuted(xs[n], (0,))
        before = []
        for k in range(left, 0, -1):
            edge = pltpu.roll(xp[(seg - k) * 8:(seg - k + 1) * 8], 1, axis=0)
            before.append(jnp.where(sub == 0, prev[HALO - k:HALO - k + 1, sl], edge))
        after = []
        for k in range(CONV_WIDTH - 1 - left):
            edge = pltpu.roll(xp[k * 8:(k + 1) * 8], 7, axis=0)
            after.append(jnp.where(sub == 7, nxt[k:k + 1, sl], edge))
        xext = jnp.concatenate(before + [xp] + after, axis=0)
        xcn = cb_ref[0][:, sl]
        for j in range(CONV_WIDTH):
            xcn = xcn + xext[8 * j:8 * j + tt] * cw[j:j + 1, sl]
        pre = _dot(xcn.astype(BF16), w_ref[0, n])
        tr = jnp.tanh(pre[:, :LRU_BLOCK] + 0.5 * ba_ref[0][:, sl])
        ig = 0.5 * jnp.tanh(pre[:, LRU_BLOCK:] + 0.5 * bx_ref[0][:, sl]) + 0.5
        c4 = (-0.5 * LRU_C) * sp[:, sl]
        log_a = c4 * tr + c4
        a = jnp.exp(log_a)
        one_m_a2 = -jnp.tanh(log_a) * (a * a + 1.0)
        root = jnp.where(one_m_a2 > 0.0, one_m_a2 * lax.rsqrt(one_m_a2), 0.0)
        a_ref[n] = a
        b_ref[n] = jnp.where(valid, root * (ig * xcn), 0.0)

    def vreg(j):
        jj = seg - 1 - j if rev else j
        return jj, pl.ds(pl.multiple_of(jj * 8, 8), 8)

    def local(j, carry):
        hs, ps = carry
        _, rows8 = vreg(j)
        nh, npr = [], []
        for n in range(LRU_BLOCKS):
            av = a_ref[n, rows8, :]
            h = av * hs[n] + b_ref[n, rows8, :]
            b_ref[n, rows8, :] = h
            nh.append(h)
            npr.append(av * ps[n])
        return tuple(nh), tuple(npr)

    zeros = tuple(jnp.zeros((8, LRU_BLOCK), F32) for _ in range(LRU_BLOCKS))
    ones = tuple(jnp.ones((8, LRU_BLOCK), F32) for _ in range(LRU_BLOCKS))
    h_end, p_end = lax.fori_loop(0, seg, local, (zeros, ones))

    cins = []
    for n in range(LRU_BLOCKS):
        c = h_ref[n]
        rows = [None] * 8
        for s in (range(7, -1, -1) if rev else range(8)):
            rows[s] = c
            c = p_end[n][s:s + 1] * c + h_end[n][s:s + 1]
        h_ref[n] = c
        cins.append(jnp.concatenate(rows, axis=0))

    if rev:
        for n in range(LRU_BLOCKS):
            g = gates[n][0]
            on_ref[n] = g * _sigmoid(g)

    def fixup(j, ps):
        jj, rows8 = vreg(j)
        strided = pl.ds(jj, 8, stride=seg)
        npr = []
        for n in range(LRU_BLOCKS):
            p = a_ref[n, rows8, :] * ps[n]
            h = b_ref[n, rows8, :] + p * cins[n]
            if rev:
                on_ref[n, strided, :] = (hf_ref[n, 0, strided, :] + h) * on_ref[n, strided, :]
            else:
                o_ref[n, 0, strided, :] = h
            npr.append(p)
        return tuple(npr)

    lax.fori_loop(0, seg, fixup, ones)
    if rev:
        for n in range(LRU_BLOCKS):
            o_ref[0, :, n * LRU_BLOCK:(n + 1) * LRU_BLOCK] = on_ref[n].astype(o_ref.dtype)


def _lru_pass(cols1, cw, cb, w, ba, bx, lam, layer, rev, tt, pad, h_fwd=None):
    bsz, lq, _ = cols1.shape
    nt = lq // tt
    d = 1 if rev else 0
    nh = lq // HALO
    th = tt // HALO
    tmap = (lambda i: nt - 1 - i) if rev else (lambda i: i)

    def slabs(col0):
        return [pl.BlockSpec((1, tt, LRU_BLOCK), functools.partial(lambda b, i, n: (b, tmap(i), n), n=col0 + n))
                for n in range(LRU_BLOCKS)]

    in_specs = slabs(C1_X // LRU_BLOCK) + [
        pl.BlockSpec((1, HALO, LRU_WIDTH), lambda b, i: (b, jnp.maximum(tmap(i) * th - 1, 0), 0)),
        pl.BlockSpec((1, HALO, LRU_WIDTH), lambda b, i: (b, jnp.minimum((tmap(i) + 1) * th, nh - 1), 0)),
        pl.BlockSpec((1, 8, LRU_WIDTH), lambda b, i: (layer, 0, 0)),
        pl.BlockSpec((1, 1, LRU_WIDTH), lambda b, i: (layer, 0, 0)),
        pl.BlockSpec((1, LRU_BLOCKS, LRU_BLOCK, 2 * LRU_BLOCK), lambda b, i: (layer * 2 + d, 0, 0, 0)),
        pl.BlockSpec((1, 1, LRU_WIDTH), lambda b, i: (layer * 2 + d, 0, 0)),
        pl.BlockSpec((1, 1, LRU_WIDTH), lambda b, i: (layer * 2 + d, 0, 0)),
        pl.BlockSpec((1, 1, LRU_WIDTH), lambda b, i: (layer * 2 + d, 0, 0)),
    ]
    args = [cols1] * (LRU_BLOCKS + 2) + [cw, cb, w, ba, bx, lam]
    scratch = [
        pltpu.VMEM((LRU_BLOCKS, tt, LRU_BLOCK), F32),
        pltpu.VMEM((LRU_BLOCKS, tt, LRU_BLOCK), F32),
        pltpu.VMEM((LRU_BLOCKS, 1, LRU_BLOCK), F32),
    ]
    if rev:
        in_specs += [pl.BlockSpec((LRU_BLOCKS, 1, tt, LRU_BLOCK), lambda b, i: (0, b, tmap(i), 0))]
        in_specs += slabs(C1_GC // LRU_BLOCK)
        args += [h_fwd] + [cols1] * LRU_BLOCKS
        scratch += [pltpu.VMEM((LRU_BLOCKS, tt, LRU_BLOCK), F32)]
        out_spec = pl.BlockSpec((1, tt, LRU_WIDTH), lambda b, i: (b, tmap(i), 0))
        out_shape = jax.ShapeDtypeStruct((bsz, lq, LRU_WIDTH), BF16)
    else:
        out_spec = pl.BlockSpec((LRU_BLOCKS, 1, tt, LRU_BLOCK), lambda b, i: (0, b, tmap(i), 0))
        out_shape = jax.ShapeDtypeStruct((LRU_BLOCKS, bsz, lq, LRU_BLOCK), F32)
    return pl.pallas_call(
        functools.partial(_lru_kernel, rev=rev, tt=tt, nt=nt, pad=pad),
        grid=(bsz, nt),
        in_specs=in_specs,
        out_specs=out_spec,
        out_shape=out_shape,
        scratch_shapes=scratch,
        compiler_params=_params(("parallel", "arbitrary")),
        name="lru_bwd" if rev else "lru_fwd",
    )(*args)


def _out_kernel(z_ref, g_ref, mask_ref, ys_ref, ga_ref, yb_ref, yc_ref, wglu_ref, bglu_ref,
                wm0_ref, wm1_ref, wm2_ref, woa_ref, wob_ref, woc_ref, wo_ref, fg_ref,
                o_ref, h_ref, ya_ref, *, final):
    j = pl.program_id(1)
    nj = pl.num_programs(1)

    @pl.when(j == 0)
    def _():
        z = z_ref[...]
        ms = jnp.mean(z * z, axis=-1, keepdims=True)
        h_ref[...] = (z * lax.rsqrt(ms + EPS) * g_ref[0]).astype(BF16)
        zz = jax.nn.gelu(ys_ref[...])
        glu = _sigmoid(_dot(zz.astype(BF16), wglu_ref[0]) + bglu_ref[0])
        ga = ga_ref[...]
        ya_ref[...] = (zz * glu * (ga * _sigmoid(ga))).astype(BF16)

    h = h_ref[...]
    m = (_sigmoid(_dot(h, wm0_ref[0])) * _dot(ya_ref[...], woa_ref[0])
         + _sigmoid(_dot(h, wm1_ref[0])) * _dot(yb_ref[...], wob_ref[0])
         + _sigmoid(_dot(h, wm2_ref[0])) * _dot(yc_ref[...], woc_ref[0]))
    contrib = _dot(m.astype(BF16), wo_ref[0])

    @pl.when(j == 0)
    def _():
        o_ref[...] = contrib

    @pl.when(jnp.logical_and(j > 0, j < nj - 1))
    def _():
        o_ref[...] += contrib

    @pl.when(j == nj - 1)
    def _():
        zn = z_ref[...] + (o_ref[...] + contrib)
        if final:
            ms = jnp.mean(zn * zn, axis=-1, keepdims=True)
            o_ref[...] = zn * lax.rsqrt(ms + EPS) * fg_ref[...]
        else:
            o_ref[...] = jnp.where(mask_ref[...] > 0.0, zn, 0.0)


def _out_proj(z, g, mask, cols1, ys, yb, yc, wglu, bglu, wm, woa, wob, woc, wo, fg, layer, tm, final):
    r, d = z.shape
    TJ = TJ_WIDE if tm <= OUT_TILE else TJ_NARROW
    nj = d // TJ
    return pl.pallas_call(
        functools.partial(_out_kernel, final=final),
        grid=(r // tm, nj),
        in_specs=[
            pl.BlockSpec((tm, d), lambda i, j: (i, 0)),
            pl.BlockSpec((1, 1, d), lambda i, j: (layer, 0, 0)),
            pl.BlockSpec((tm, 1), lambda i, j: (i, 0)),
            pl.BlockSpec((tm, S5_WIDTH), lambda i, j: (i, 0)),
            pl.BlockSpec((tm, S5_WIDTH), lambda i, j: (i, C1_GA // S5_WIDTH)),
            pl.BlockSpec((tm, GLA_WIDTH), lambda i, j: (i, 0)),
            pl.BlockSpec((tm, LRU_WIDTH), lambda i, j: (i, 0)),
            pl.BlockSpec((1, S5_WIDTH, S5_WIDTH), lambda i, j: (layer, 0, 0)),
            pl.BlockSpec((1, 1, S5_WIDTH), lambda i, j: (layer, 0, 0)),
            pl.BlockSpec((1, d, TJ), lambda i, j: (layer, 0, j)),
            pl.BlockSpec((1, d, TJ), lambda i, j: (layer, 0, nj + j)),
            pl.BlockSpec((1, d, TJ), lambda i, j: (layer, 0, 2 * nj + j)),
            pl.BlockSpec((1, S5_WIDTH, TJ), lambda i, j: (layer, 0, j)),
            pl.BlockSpec((1, GLA_WIDTH, TJ), lambda i, j: (layer, 0, j)),
            pl.BlockSpec((1, LRU_WIDTH, TJ), lambda i, j: (layer, 0, j)),
            pl.BlockSpec((1, TJ, d), lambda i, j: (layer, j, 0)),
            pl.BlockSpec((1, d), lambda i, j: (0, 0)),
        ],
        out_specs=pl.BlockSpec((tm, d), lambda i, j: (i, 0)),
        out_shape=jax.ShapeDtypeStruct((r, d), F32),
        scratch_shapes=[
            pltpu.VMEM((tm, d), BF16),
            pltpu.VMEM((tm, S5_WIDTH), BF16),
        ],
        compiler_params=_params(("parallel", "arbitrary")),
        name="out_proj",
    )(z, g, mask, ys, cols1, yb, yc, wglu, bglu, wm, wm, wm, woa, wob, woc, wo, fg)


def _prepare(p):
    depth = p["w_in"].shape[0]
    w_in = p["w_in"]
    o = 0
    seg = {}
    for name, width in (("u", 512), ("ga", 512), ("q", 256), ("k", 256), ("v", 512), ("gb", 512),
                        ("glr", 2 * GLA_RANK), ("x", 1024), ("gc", 1024), ("m", 3 * w_in.shape[1])):
        seg[name] = w_in[:, :, o:o + width]
        o += width
    w1 = jnp.concatenate([seg["x"], seg["gc"], seg["u"], seg["ga"]], axis=-1).astype(BF16)
    zpad = jnp.zeros(w_in.shape[:2] + (N2 - C2_GLR - 2 * GLA_RANK,), w_in.dtype)
    w2 = jnp.concatenate([seg["v"], seg["gb"], seg["q"], seg["k"], seg["glr"], zpad], axis=-1).astype(BF16)
    wm = seg["m"].astype(BF16)

    ng = S5_WIDTH // 128
    eye = jnp.eye(S5_GB, dtype=F32)

    def rows(a):
        return a.astype(F32).reshape(depth, 2, ng, S5_NS)

    ls = jnp.broadcast_to(p["s5_log_step"].astype(F32)[..., None], p["s5_lam_re"].shape)
    lr, li, ls = rows(p["s5_lam_re"]), rows(p["s5_lam_im"]), rows(ls)
    zero = jnp.zeros_like(lr[:, 0])
    prm = jnp.stack([lr[:, 0], li[:, 0], ls[:, 0], lr[:, 1], li[:, 1], ls[:, 1], zero, zero], axis=2)

    def emb_b(bm):
        bm = bm.astype(F32).reshape(depth, ng, S5_GB, 64, 16)
        return jnp.einsum("lGgnc,gh->lGgchn", bm, eye).reshape(depth, ng, 128, S5_NS)

    def emb_c(cm):
        cm = cm.astype(F32).reshape(depth, 2, ng, S5_GB, 16, 64)
        return jnp.einsum("ldGgcn,gh->ldGgchn", cm, eye).reshape(depth, 2, ng, 128, S5_NS)

    wg = jnp.zeros((depth, 2, 128, GLA_KEY), F32)
    for d in range(2):
        wg = wg.at[:, d, d * GLA_RANK:(d + 1) * GLA_RANK, :].set(p["gla_w_gate_up"][:, d].astype(F32))
    lru_w = (0.5 * jnp.concatenate([p["lru_w_a"], p["lru_w_x"]], axis=-1)).astype(BF16)
    cw = jnp.concatenate([p["conv_w"].astype(F32),
                          jnp.zeros((depth, 8 - CONV_WIDTH, LRU_WIDTH), F32)], axis=1)
    return dict(
        norm_g=p["norm_g"].astype(F32).reshape(depth, 1, -1), w1=w1, w2=w2, wm=wm,
        s5_prm=prm, s5_lbr=emb_b(p["s5_b_re"]), s5_lbi=emb_b(p["s5_b_im"]),
        s5_ctr=emb_c(p["s5_c_re"]), s5_cti=emb_c(p["s5_c_im"]),
        s5_d=p["s5_d"].astype(F32).reshape(depth, ng, 1, 128),
        wglu=p["s5_w_glu"].astype(BF16), bglu=p["s5_b_glu"].astype(F32).reshape(depth, 1, S5_WIDTH),
        wg=wg.reshape(depth * 2, 128, GLA_KEY).astype(BF16),
        bg=p["gla_b_gate"].astype(F32).reshape(depth * 2, 1, GLA_KEY),
        gla_ng=p["gla_norm_g"].astype(F32).reshape(depth, 1, GLA_WIDTH),
        cw=cw, cb=p["conv_b"].astype(F32).reshape(depth, 1, LRU_WIDTH),
        lru_w=lru_w.reshape(depth * 2, LRU_BLOCKS, LRU_BLOCK, 2 * LRU_BLOCK),
        lru_ba=p["lru_b_a"].astype(F32).reshape(depth * 2, 1, LRU_WIDTH),
        lru_bx=p["lru_b_x"].astype(F32).reshape(depth * 2, 1, LRU_WIDTH),
        lru_lam=p["lru_lam"].astype(F32).reshape(depth * 2, 1, LRU_WIDTH),
        woa=p["w_out_a"].astype(BF16), wob=p["w_out_b"].astype(BF16), woc=p["w_out_c"].astype(BF16),
        wo=p["w_o"].astype(BF16), fg=p["final_norm_g"].astype(F32).reshape(1, -1),
    )


def _pick_tile(n, unit, target):
    best = unit
    for t in range(unit, target + 1, unit):
        if n % t == 0:
            best = t
    return best


def _pick_lru_tile(lq, target):
    best = None
    for seg in range(1, target // 8 + 1):
        if lq % (8 * seg) == 0 and seg % 8 != 0:
            best = 8 * seg
    assert best is not None
    return best


def _encoder(x, meta, w):
    bsz, seq, d = x.shape
    depth = w["w2"].shape[0]
    pad = PAD
    while (bsz * (pad + N_META + seq)) % OUT_TILE != 0 and pad < PAD + 4 * GLA_CHUNK:
        pad += GLA_CHUNK
    if (bsz * (pad + N_META + seq)) % OUT_TILE != 0:
        pad = PAD
    lq = pad + N_META + seq
    r = bsz * lq
    head = jnp.concatenate([jnp.zeros((pad, d), F32), meta.astype(F32)], axis=0)
    z = jnp.concatenate([jnp.broadcast_to(head[None], (bsz, pad + N_META, d)), x.astype(F32)], axis=1)
    z = z.reshape(r, d)
    mask = jnp.broadcast_to((jnp.arange(lq) >= pad).astype(F32)[None, :, None], (bsz, lq, 1)).reshape(r, 1)
    tm = _pick_tile(r, LANES, IN_TILE_MAX)
    tm_out = _pick_tile(r, LANES, OUT_TILE_MAX)
    cpt = _pick_tile(lq // GLA_CHUNK, 1, GLA_CHUNKS_MAX)
    tt = _pick_lru_tile(lq, LRU_TILE_MAX)
    for l in range(depth):
        cols1, cols2 = _in_proj(z, w["norm_g"], w["w1"], w["w2"], l, tm)
        c1 = cols1.reshape(bsz, lq, N1)
        c2 = cols2.reshape(bsz, lq, N2)
        ys = _s5(c1, w["s5_prm"], w["s5_lbr"], w["s5_lbi"], w["s5_ctr"], w["s5_cti"], w["s5_d"], l)
        of = _gla_pass(c2, w["wg"], w["bg"], l, False, cpt, pad)
        yb = _gla_pass(c2, w["wg"], w["bg"], l, True, cpt, pad, of, w["gla_ng"])
        lru_w = (w["cw"], w["cb"], w["lru_w"], w["lru_ba"], w["lru_bx"], w["lru_lam"])
        hf = _lru_pass(c1, *lru_w, l, False, tt, pad)
        yc = _lru_pass(c1, *lru_w, l, True, tt, pad, hf)
        z = _out_proj(z, w["norm_g"], mask, cols1, ys.reshape(r, -1), yb.reshape(r, -1),
                      yc.reshape(r, -1), w["wglu"], w["bglu"], w["wm"], w["woa"], w["wob"], w["woc"],
                      w["wo"], w["fg"], l, tm_out, l == depth - 1)
    return z.reshape(bsz, lq, d)[:, pad + N_META:].astype(x.dtype)


def kernel(x_prompt, x_sample, meta_tokens, norm_g, w_in, s5_lam_re, s5_lam_im, s5_log_step, s5_b_re, s5_b_im, s5_c_re, s5_c_im, s5_d, s5_w_glu, s5_b_glu, gla_w_gate_up, gla_b_gate, gla_norm_g, conv_w, conv_b, lru_w_a, lru_b_a, lru_w_x, lru_b_x, lru_lam, w_out_a, w_out_b, w_out_c, w_o, final_norm_g):
    w = _prepare(dict(
        norm_g=norm_g, w_in=w_in, s5_lam_re=s5_lam_re, s5_lam_im=s5_lam_im, s5_log_step=s5_log_step,
        s5_b_re=s5_b_re, s5_b_im=s5_b_im, s5_c_re=s5_c_re, s5_c_im=s5_c_im, s5_d=s5_d,
        s5_w_glu=s5_w_glu, s5_b_glu=s5_b_glu, gla_w_gate_up=gla_w_gate_up, gla_b_gate=gla_b_gate,
        gla_norm_g=gla_norm_g, conv_w=conv_w, conv_b=conv_b, lru_w_a=lru_w_a, lru_b_a=lru_b_a,
        lru_w_x=lru_w_x, lru_b_x=lru_b_x, lru_lam=lru_lam, w_out_a=w_out_a, w_out_b=w_out_b,
        w_out_c=w_out_c, w_o=w_o, final_norm_g=final_norm_g))
    return (_encoder(x_prompt, meta_tokens, w), _encoder(x_sample, meta_tokens, w))
```

```python
import functools

import jax
import jax.numpy as jnp
from jax import lax
from jax.experimental import pallas as pl
from jax.experimental.pallas import tpu as pltpu

F32 = jnp.float32
BF16 = jnp.bfloat16
HIGHEST = lax.Precision.HIGHEST

N_META = 16
PAD = 48
EPS = 1e-6
S5_WIDTH = 512
S5_T = 8
S5_GB = 8
S5_NS = 512
GLA_HEADS = 4
GLA_DK = 64
GLA_DV = 128
GLA_KEY = 256
GLA_WIDTH = 512
GLA_RANK = 16
GLA_GATE_NORM = 16.0
GLA_CHUNK = 64
LRU_WIDTH = 1024
LRU_BLOCKS = 8
LRU_BLOCK = 128
CONV_WIDTH = 4
LRU_C = 8.0
HALO = 8
LANES = 128
VMEM_LIMIT = 56 * 1024 * 1024
IN_TILE_MAX = 768
OUT_TILE_MAX = 640
OUT_TILE = 512
GLA_CHUNKS_MAX = 17
LRU_TILE_MAX = 640

C1_X, C1_GC, C1_U, C1_GA = 0, 1024, 2048, 2560
N1 = 3072
C2_V, C2_GB, C2_Q, C2_K, C2_GLR = 0, 512, 1024, 1280, 1536
N2 = 1664
TN = 1024
NJ1 = N1 // TN
TJ_WIDE, TJ_NARROW = 512, 256


def _dot(a, b):
    return jnp.dot(a, b, preferred_element_type=F32)


def _dot_nt(a, b):
    return lax.dot_general(a, b, (((1,), (1,)), ((), ())), preferred_element_type=F32)


def _dot_tn(a, b):
    return lax.dot_general(a, b, (((0,), (0,)), ((), ())), preferred_element_type=F32)


def _sigmoid(x):
    return 0.5 * jnp.tanh(0.5 * x) + 0.5


def _params(sem):
    return pltpu.CompilerParams(dimension_semantics=sem, vmem_limit_bytes=VMEM_LIMIT)


def _kin_kernel(z_ref, g_ref, w1_hbm, w2_hbm, o1_ref, o2_ref, h_ref, w1_buf, w2_buf, sem, *, layer):
    i = pl.program_id(0)
    j = pl.program_id(1)

    def w1_copy(k):
        return pltpu.make_async_copy(w1_hbm.at[layer, :, pl.ds(k * TN, TN)], w1_buf.at[k], sem.at[k])

    def w2_copy():
        return pltpu.make_async_copy(w2_hbm.at[layer], w2_buf, sem.at[NJ1])

    @pl.when(jnp.logical_and(i == 0, j == 0))
    def _():
        for k in range(NJ1):
            w1_copy(k).start()
        w2_copy().start()

    @pl.when(j == 0)
    def _():
        z = z_ref[...]
        ms = jnp.mean(z * z, axis=-1, keepdims=True)
        h_ref[...] = (z * lax.rsqrt(ms + EPS) * g_ref[0]).astype(BF16)

    for k in range(NJ1):
        @pl.when(jnp.logical_and(i == 0, j == k))
        def _(k=k):
            w1_copy(k).wait()

        @pl.when(j == k)
        def _(k=k):
            o1_ref[...] = _dot(h_ref[...], w1_buf[k])

    @pl.when(jnp.logical_and(i == 0, j == NJ1))
    def _():
        w2_copy().wait()

    @pl.when(j == NJ1)
    def _():
        o2_ref[...] = _dot(h_ref[...], w2_buf[...]).astype(o2_ref.dtype)


def _in_proj(z, g, w1, w2, layer, tm):
    r, d = z.shape
    last = NJ1 - 1
    return pl.pallas_call(
        functools.partial(_kin_kernel, layer=layer),
        grid=(r // tm, NJ1 + 1),
        in_specs=[
            pl.BlockSpec((tm, d), lambda i, j: (i, 0)),
            pl.BlockSpec((1, 1, d), lambda i, j: (layer, 0, 0)),
            pl.BlockSpec(memory_space=pl.ANY),
            pl.BlockSpec(memory_space=pl.ANY),
        ],
        out_specs=[
            pl.BlockSpec((tm, TN), lambda i, j: (i, jnp.minimum(j, last))),
            pl.BlockSpec((tm, N2), lambda i, j: (i, 0)),
        ],
        out_shape=[jax.ShapeDtypeStruct((r, N1), F32), jax.ShapeDtypeStruct((r, N2), BF16)],
        scratch_shapes=[
            pltpu.VMEM((tm, d), BF16),
            pltpu.VMEM((NJ1, d, TN), BF16),
            pltpu.VMEM((d, N2), BF16),
            pltpu.SemaphoreType.DMA((NJ1 + 1,)),
        ],
        compiler_params=_params(("arbitrary", "arbitrary")),
        name="in_proj",
    )(z, g, w1, w2)


def _s5_kernel(u_ref, prm_ref, lbr_ref, lbi_ref, ctr_ref, cti_ref, dsk_ref, o_ref,
               wst_ref, m_ref, wout_ref, e_ref, w_ref, s_ref, apow_ref, x_ref, *, nc, bsub):
    b = pl.program_id(1)
    ns = S5_NS
    blk = S5_GB * 16
    ncp = -(-nc // 8) * 8

    @pl.when(b == 0)
    def _build():
        prm = prm_ref[0, 0]
        evec = lax.broadcasted_iota(jnp.int32, (24, 1), 0).astype(F32)
        e_ref[...] = jnp.zeros_like(e_ref)
        for d in range(2):
            lr = prm[3 * d:3 * d + 1]
            li = prm[3 * d + 1:3 * d + 2]
            dt = jnp.exp(prm[3 * d + 2:3 * d + 3])
            mag = jnp.exp(evec * (lr * dt))
            ang = evec * (li * dt)
            pre = mag * jnp.cos(ang)
            pim = mag * jnp.sin(ang)
            abr = pre[1:2]
            abi = pim[1:2]
            den = lr * lr + li * li
            fr = ((abr - 1.0) * lr + abi * li) / den
            fi = (abi * lr - (abr - 1.0) * li) / den
            lbr = lbr_ref[0, 0]
            lbi = lbi_ref[0, 0]
            bbr = fr * lbr - fi * lbi
            bbi = fr * lbi + fi * lbr
            ctr = ctr_ref[0, d, 0]
            cti = cti_ref[0, d, 0]
            apow_ref[2 * d:2 * d + 1, :] = pre[S5_T:S5_T + 1]
            apow_ref[2 * d + 1:2 * d + 2, :] = pim[S5_T:S5_T + 1]
            sr, si = jnp.ones_like(lr), jnp.zeros_like(lr)
            br_, bi_ = pre[S5_T:S5_T + 1], pim[S5_T:S5_T + 1]
            nseg = ncp // 8
            while nseg:
                if nseg & 1:
                    sr, si = sr * br_ - si * bi_, sr * bi_ + si * br_
                br_, bi_ = br_ * br_ - bi_ * bi_, 2.0 * br_ * bi_
                nseg >>= 1
            apow_ref[4 + 2 * d:5 + 2 * d, :] = sr
            apow_ref[5 + 2 * d:6 + 2 * d, :] = si
            for r in range(S5_T):
                e = S5_T - 1 - r if d == 0 else r
                pr_e = pre[e:e + 1]
                pi_e = pim[e:e + 1]
                l_re = bbr * pr_e - bbi * pi_e
                l_im = bbr * pi_e + bbi * pr_e
                base = 2 * d * ns
                wst_ref[r * blk:(r + 1) * blk, base:base + ns] = l_re.astype(BF16)
                wst_ref[r * blk:(r + 1) * blk, base + ns:base + 2 * ns] = l_im.astype(BF16)
                kblk = _dot_nt(l_re.astype(BF16), ctr.astype(BF16)) - _dot_nt(l_im.astype(BF16), cti.astype(BF16))
                eb = r if d == 0 else S5_T - 1 + r
                e_ref[eb * blk:(eb + 1) * blk, :] += kblk
                eo = r + 1 if d == 0 else S5_T - r
                pr_o = pre[eo:eo + 1]
                pi_o = pim[eo:eo + 1]
                wout_ref[r * blk:(r + 1) * blk, base:base + ns] = (pr_o * ctr - pi_o * cti).astype(BF16)
                wout_ref[r * blk:(r + 1) * blk, base + ns:base + 2 * ns] = (
                    -(pi_o * ctr) - pr_o * cti).astype(BF16)
        for t in range(S5_T):
            off = (S5_T - 1 - t) * blk
            m_ref[:, t * blk:(t + 1) * blk] = e_ref[off:off + S5_T * blk, :].astype(BF16)

    for bb in range(bsub):
        for t in range(S5_T):
            x_ref[bb * ncp:bb * ncp + nc, t * blk:(t + 1) * blk] = u_ref[bb, pl.ds(t, nc, stride=S5_T), :]
        if ncp > nc:
            x_ref[bb * ncp + nc:(bb + 1) * ncp, :] = jnp.zeros((ncp - nc, S5_T * blk), F32)
    x = x_ref[...]
    xb = x.astype(BF16)
    wv = _dot(xb, wst_ref[...])
    nq = ns // 128
    for kk in range(4 * nq):
        w_ref[kk] = wv[:, kk * 128:(kk + 1) * 128]

    seg = ncp // 8
    apow = apow_ref[...]
    pairs = [(d, q) for d in range(2) for q in range(nq)]

    def coef(row, q):
        return apow[row:row + 1, q * 128:(q + 1) * 128]

    zero8 = jnp.zeros((8, 128), F32)
    one8 = jnp.ones((8, 128), F32)

    def scan_sequence(base):
        def rows(d, q, j):
            jj = base + (j if d == 0 else seg - 1 - j)
            return ((2 * d * nq + q, pl.ds(jj, 8, stride=seg), slice(None)),
                    ((2 * d + 1) * nq + q, pl.ds(jj, 8, stride=seg), slice(None)))

        def local(j, carry):
            new = []
            for n, (d, q) in enumerate(pairs):
                cr, ci = carry[2 * n], carry[2 * n + 1]
                ir, ii = rows(d, q, j)
                ar, ai = coef(2 * d, q), coef(2 * d + 1, q)
                s_ref[ir] = cr
                s_ref[ii] = ci
                new += [ar * cr - ai * ci + w_ref[ir], ar * ci + ai * cr + w_ref[ii]]
            return tuple(new)

        totals = lax.fori_loop(0, seg, local, tuple(zero8 for _ in range(2 * len(pairs))))

        cins = []
        for n, (d, q) in enumerate(pairs):
            tr, ti = totals[2 * n], totals[2 * n + 1]
            sr, si = coef(4 + 2 * d, q), coef(5 + 2 * d, q)
            cr = jnp.zeros((1, 128), F32)
            ci = jnp.zeros((1, 128), F32)
            rr, ri = [None] * 8, [None] * 8
            for s in (range(8) if d == 0 else range(7, -1, -1)):
                rr[s], ri[s] = cr, ci
                cr, ci = sr * cr - si * ci + tr[s:s + 1], sr * ci + si * cr + ti[s:s + 1]
            cins += [jnp.concatenate(rr, axis=0), jnp.concatenate(ri, axis=0)]

        def fixup(j, ps):
            new = []
            for n, (d, q) in enumerate(pairs):
                pr, pi = ps[2 * n], ps[2 * n + 1]
                ir, ii = rows(d, q, j)
                cr, ci = cins[2 * n], cins[2 * n + 1]
                s_ref[ir] = s_ref[ir] + (pr * cr - pi * ci)
                s_ref[ii] = s_ref[ii] + (pr * ci + pi * cr)
                ar, ai = coef(2 * d, q), coef(2 * d + 1, q)
                new += [ar * pr - ai * pi, ar * pi + ai * pr]
            return tuple(new)

        lax.fori_loop(0, seg, fixup, tuple(one8 if n % 2 == 0 else zero8 for n in range(2 * len(pairs))))

    for bb in range(bsub):
        scan_sequence(bb * ncp)

    s_in = jnp.concatenate([s_ref[kk] for kk in range(4 * nq)], axis=1).astype(BF16)
    y = (_dot(xb, m_ref[...]) + _dot_nt(s_in, wout_ref[...])
         + x * jnp.concatenate([dsk_ref[0, 0]] * S5_T, axis=1))
    for bb in range(bsub):
        for t in range(S5_T):
            o_ref[bb, pl.ds(t, nc, stride=S5_T), :] = y[bb * ncp:bb * ncp + nc, t * blk:(t + 1) * blk]


def _s5(cols1, prm, lbr, lbi, ctr, cti, dsk, layer):
    bsz, lq, _ = cols1.shape
    nc = lq // S5_T
    ncp = -(-nc // 8) * 8
    ng = S5_WIDTH // LANES
    n = S5_T * LANES
    ublk = C1_U // LANES
    bsub = 2 if (bsz % 2 == 0 and 2 * ncp <= 640) else 1
    return pl.pallas_call(
        functools.partial(_s5_kernel, nc=nc, bsub=bsub),
        grid=(ng, bsz // bsub),
        in_specs=[
            pl.BlockSpec((bsub, lq, 128), lambda g, b: (b, 0, ublk + g)),
            pl.BlockSpec((1, 1, 8, S5_NS), lambda g, b: (layer, g, 0, 0)),
            pl.BlockSpec((1, 1, 128, S5_NS), lambda g, b: (layer, g, 0, 0)),
            pl.BlockSpec((1, 1, 128, S5_NS), lambda g, b: (layer, g, 0, 0)),
            pl.BlockSpec((1, 2, 1, 128, S5_NS), lambda g, b: (layer, 0, g, 0, 0)),
            pl.BlockSpec((1, 2, 1, 128, S5_NS), lambda g, b: (layer, 0, g, 0, 0)),
            pl.BlockSpec((1, 1, 1, 128), lambda g, b: (layer, g, 0, 0)),
        ],
        out_specs=pl.BlockSpec((bsub, lq, 128), lambda g, b: (b, 0, g)),
        out_shape=jax.ShapeDtypeStruct((bsz, lq, S5_WIDTH), F32),
        scratch_shapes=[
            pltpu.VMEM((n, 4 * S5_NS), BF16),
            pltpu.VMEM((n, n), BF16),
            pltpu.VMEM((n, 4 * S5_NS), BF16),
            pltpu.VMEM(((2 * S5_T - 1) * 128, 128), F32),
            pltpu.VMEM((4 * S5_NS // 128, bsub * ncp, 128), F32),
            pltpu.VMEM((4 * S5_NS // 128, bsub * ncp, 128), F32),
            pltpu.VMEM((8, S5_NS), F32),
            pltpu.VMEM((bsub * ncp, n), F32),
        ],
        compiler_params=_params(("arbitrary", "arbitrary")),
        name="s5_mixer",
    )(cols1, prm, lbr, lbi, ctr, cti, dsk)


def _gla_kernel(*refs, rev, cpt, nt, pad):
    if rev:
        (q_ref, k_ref, v_ref, glr_ref, wg_ref, bg_ref, of_ref, gate_ref, ng_ref, o_ref, st_ref) = refs
    else:
        (q_ref, k_ref, v_ref, glr_ref, wg_ref, bg_ref, o_ref, st_ref) = refs
    i = pl.program_id(1)
    tile = nt - 1 - i if rev else i
    ch = GLA_CHUNK

    @pl.when(i == 0)
    def _():
        st_ref[...] = jnp.zeros_like(st_ref)

    tt = cpt * ch
    row = lax.broadcasted_iota(jnp.int32, (ch, ch), 0)
    col = lax.broadcasted_iota(jnp.int32, (ch, ch), 1)
    causal = (col >= row) if rev else (col <= row)
    tri = jnp.where(causal, 1.0, 0.0).astype(F32)

    glin = _dot(glr_ref[0].astype(BF16), wg_ref[0]) + bg_ref[0]
    g = jax.nn.log_sigmoid(glin) / GLA_GATE_NORM
    t_glob = tile * tt + lax.broadcasted_iota(jnp.int32, (tt, 1), 0)
    g = jnp.where(t_glob >= pad, g, 0.0)
    bcs = [jnp.dot(tri, g[c * ch:(c + 1) * ch], precision=HIGHEST, preferred_element_type=F32)
           for c in range(cpt)]
    blasts = [bc[0:1] if rev else bc[ch - 1:ch] for bc in bcs]
    bcum = jnp.concatenate(bcs, axis=0)
    brel = jnp.concatenate([blasts[c] - bcs[c] for c in range(cpt)], axis=0)
    k = k_ref[0].astype(F32)
    qe = (q_ref[0].astype(F32) * (GLA_DK ** -0.5) * jnp.exp(bcum)).astype(BF16)
    ke = (k * jnp.exp(-bcum)).astype(BF16)
    kd = (k * jnp.exp(brel)).astype(BF16)
    v = v_ref[0].astype(BF16)

    npair = GLA_HEADS // 2
    kw, vw = 2 * GLA_DK, 2 * GLA_DV
    hi = lambda shape, axis, shift: lax.broadcasted_iota(jnp.int32, shape, axis) >> shift
    bm_k = hi((2 * ch, kw), 0, 6) == hi((2 * ch, kw), 1, 6)
    bm_v = hi((2 * ch, vw), 0, 6) == hi((2 * ch, vw), 1, 7)
    bm_s = hi((vw, kw), 0, 7) == hi((vw, kw), 1, 6)
    arow = lax.broadcasted_iota(jnp.int32, (ch, 2 * ch), 0)
    acol = lax.broadcasted_iota(jnp.int32, (ch, 2 * ch), 1) & (ch - 1)
    causal2 = (acol >= arow) if rev else (acol <= arow)
    zk = jnp.zeros((), BF16)
    states = [st_ref[p] for p in range(npair)]
    for ci in range(cpt):
        c = cpt - 1 - ci if rev else ci
        rs = slice(c * ch, (c + 1) * ch)
        dec = jnp.exp(blasts[c])
        outs = []
        for p in range(npair):
            ks = slice(p * kw, (p + 1) * kw)
            vs = slice(p * vw, (p + 1) * vw)
            qp = qe[rs, ks]
            ke2 = jnp.where(bm_k, jnp.concatenate([ke[rs, ks]] * 2, axis=0), zk)
            v2 = jnp.where(bm_v, jnp.concatenate([v[rs, vs]] * 2, axis=0), zk)
            att = jnp.where(causal2, _dot_nt(qp, ke2), 0.0)
            outs.append(_dot(att.astype(BF16), v2) + _dot_nt(qp, states[p].astype(BF16)))
            states[p] = states[p] * dec[:, ks] + jnp.where(bm_s, _dot_tn(v[rs, vs], kd[rs, ks]), 0.0)
        o = jnp.concatenate(outs, axis=1)
        if rev:
            o = o + of_ref[0, rs, :]
            parts = []
            for h in range(GLA_HEADS):
                oh = o[:, h * GLA_DV:(h + 1) * GLA_DV]
                parts.append(oh * lax.rsqrt(jnp.mean(oh * oh, axis=-1, keepdims=True) + EPS))
            o = jnp.concatenate(parts, axis=1) * ng_ref[0]
            gate = gate_ref[0, rs, :].astype(F32)
            o = o * (gate * _sigmoid(gate))
        o_ref[0, rs, :] = o.astype(o_ref.dtype)
    for p in range(npair):
        st_ref[p] = states[p]


def _gla_pass(cols2, wg, bg, layer, rev, cpt, pad, o_fwd=None, norm_g=None):
    bsz, lq, _ = cols2.shape
    tt = cpt * GLA_CHUNK
    nt = lq // tt
    d = 1 if rev else 0
    tmap = (lambda i: nt - 1 - i) if rev else (lambda i: i)
    in_specs = [
        pl.BlockSpec((1, tt, GLA_KEY), lambda b, i: (b, tmap(i), C2_Q // GLA_KEY)),
        pl.BlockSpec((1, tt, GLA_KEY), lambda b, i: (b, tmap(i), C2_K // GLA_KEY)),
        pl.BlockSpec((1, tt, GLA_WIDTH), lambda b, i: (b, tmap(i), C2_V // GLA_WIDTH)),
        pl.BlockSpec((1, tt, 128), lambda b, i: (b, tmap(i), C2_GLR // 128)),
        pl.BlockSpec((1, 128, GLA_KEY), lambda b, i: (layer * 2 + d, 0, 0)),
        pl.BlockSpec((1, 1, GLA_KEY), lambda b, i: (layer * 2 + d, 0, 0)),
    ]
    args = [cols2, cols2, cols2, cols2, wg, bg]
    if rev:
        in_specs += [
            pl.BlockSpec((1, tt, GLA_WIDTH), lambda b, i: (b, tmap(i), 0)),
            pl.BlockSpec((1, tt, GLA_WIDTH), lambda b, i: (b, tmap(i), C2_GB // GLA_WIDTH)),
            pl.BlockSpec((1, 1, GLA_WIDTH), lambda b, i: (layer, 0, 0)),
        ]
        args += [o_fwd, cols2, norm_g]
    return pl.pallas_call(
        functools.partial(_gla_kernel, rev=rev, cpt=cpt, nt=nt, pad=pad),
        grid=(bsz, nt),
        in_specs=in_specs,
        out_specs=pl.BlockSpec((1, tt, GLA_WIDTH), lambda b, i: (b, tmap(i), 0)),
        out_shape=jax.ShapeDtypeStruct((bsz, lq, GLA_WIDTH), BF16 if rev else F32),
        scratch_shapes=[pltpu.VMEM((GLA_HEADS // 2, 2 * GLA_DV, 2 * GLA_DK), F32)],
        compiler_params=_params(("parallel", "arbitrary")),
        name="gla_bwd" if rev else "gla_fwd",
    )(*args)


def _lru_kernel(*refs, rev, tt, nt, pad):
    xs = refs[:LRU_BLOCKS]
    refs = refs[LRU_BLOCKS:]
    if rev:
        (xp_ref, xn_ref, cw_ref, cb_ref, w_ref, ba_ref, bx_ref, lam_ref, hf_ref) = refs[:9]
        gates = refs[9:9 + LRU_BLOCKS]
        o_ref, a_ref, b_ref, h_ref, on_ref = refs[9 + LRU_BLOCKS:]
    else:
        (xp_ref, xn_ref, cw_ref, cb_ref, w_ref, ba_ref, bx_ref, lam_ref,
         o_ref, a_ref, b_ref, h_ref) = refs
    i = pl.program_id(1)
    tile = nt - 1 - i if rev else i
    seg = tt // 8

    @pl.when(i == 0)
    def _():
        h_ref[...] = jnp.zeros_like(h_ref)

    def permuted(ref, lead):
        return jnp.concatenate([ref[lead + (pl.ds(j, 8, stride=seg), slice(None))] for j in range(seg)], axis=0)

    prev = jnp.where(tile == 0, 0.0, xp_ref[0])
    nxt = jnp.where(tile == nt - 1, 0.0, xn_ref[0])
    sub = lax.broadcasted_iota(jnp.int32, (8, LRU_BLOCK), 0)
    left = CONV_WIDTH // 2
    cw = cw_ref[0]
    sp = jax.nn.softplus(-lam_ref[0])
    rperm = lax.broadcasted_iota(jnp.int32, (tt, 1), 0)
    valid = (tile * tt + (rperm & 7) * seg + (rperm >> 3)) >= pad
    for n in range(LRU_BLOCKS):
        sl = slice(n * LRU_BLOCK, (n + 1) * LRU_BLOCK)
        xp = permuted(xs[n], (0,))
        before = []
        for k in range(left, 0, -1):
            edge = pltpu.roll(xp[(seg - k) * 8:(seg - k + 1) * 8], 1, axis=0)
            before.append(jnp.where(sub == 0, prev[HALO - k:HALO - k + 1, sl], edge))
        after = []
        for k in range(CONV_WIDTH - 1 - left):
            edge = pltpu.roll(xp[k * 8:(k + 1) * 8], 7, axis=0)
            after.append(jnp.where(sub == 7, nxt[k:k + 1, sl], edge))
        xext = jnp.concatenate(before + [xp] + after, axis=0)
        xcn = cb_ref[0][:, sl]
        for j in range(CONV_WIDTH):
            xcn = xcn + xext[8 * j:8 * j + tt] * cw[j:j + 1, sl]
        pre = _dot(xcn.astype(BF16), w_ref[0, n])
        tr = jnp.tanh(pre[:, :LRU_BLOCK] + 0.5 * ba_ref[0][:, sl])
        ig = 0.5 * jnp.tanh(pre[:, LRU_BLOCK:] + 0.5 * bx_ref[0][:, sl]) + 0.5
        c4 = (-0.5 * LRU_C) * sp[:, sl]
        log_a = c4 * tr + c4
        a = jnp.exp(log_a)
        one_m_a2 = -jnp.tanh(log_a) * (a * a + 1.0)
        root = jnp.where(one_m_a2 > 0.0, one_m_a2 * lax.rsqrt(one_m_a2), 0.0)
        a_ref[n] = a
        b_ref[n] = jnp.where(valid, root * (ig * xcn), 0.0)

    def vreg(j):
        jj = seg - 1 - j if rev else j
        return jj, pl.ds(pl.multiple_of(jj * 8, 8), 8)

    def local(j, carry):
        hs, ps = carry
        _, rows8 = vreg(j)
        nh, npr = [], []
        for n in range(LRU_BLOCKS):
            av = a_ref[n, rows8, :]
            h = av * hs[n] + b_ref[n, rows8, :]
            b_ref[n, rows8, :] = h
            nh.append(h)
            npr.append(av * ps[n])
        return tuple(nh), tuple(npr)

    zeros = tuple(jnp.zeros((8, LRU_BLOCK), F32) for _ in range(LRU_BLOCKS))
    ones = tuple(jnp.ones((8, LRU_BLOCK), F32) for _ in range(LRU_BLOCKS))
    h_end, p_end = lax.fori_loop(0, seg, local, (zeros, ones))

    cins = []
    for n in range(LRU_BLOCKS):
        c = h_ref[n]
        rows = [None] * 8
        for s in (range(7, -1, -1) if rev else range(8)):
            rows[s] = c
            c = p_end[n][s:s + 1] * c + h_end[n][s:s + 1]
        h_ref[n] = c
        cins.append(jnp.concatenate(rows, axis=0))

    def fixup(j, ps):
        jj, rows8 = vreg(j)
        strided = pl.ds(jj, 8, stride=seg)
        npr = []
        for n in range(LRU_BLOCKS):
            p = a_ref[n, rows8, :] * ps[n]
            h = b_ref[n, rows8, :] + p * cins[n]
            if rev:
                g = gates[n][0, strided, :]
                on_ref[n, strided, :] = (hf_ref[n, 0, strided, :] + h) * (g * _sigmoid(g))
            else:
                o_ref[n, 0, strided, :] = h
            npr.append(p)
        return tuple(npr)

    lax.fori_loop(0, seg, fixup, ones)
    if rev:
        for n in range(LRU_BLOCKS):
            o_ref[0, :, n * LRU_BLOCK:(n + 1) * LRU_BLOCK] = on_ref[n].astype(o_ref.dtype)


def _lru_pass(cols1, cw, cb, w, ba, bx, lam, layer, rev, tt, pad, h_fwd=None):
    bsz, lq, _ = cols1.shape
    nt = lq // tt
    d = 1 if rev else 0
    nh = lq // HALO
    th = tt // HALO
    tmap = (lambda i: nt - 1 - i) if rev else (lambda i: i)

    def slabs(col0):
        return [pl.BlockSpec((1, tt, LRU_BLOCK), functools.partial(lambda b, i, n: (b, tmap(i), n), n=col0 + n))
                for n in range(LRU_BLOCKS)]

    in_specs = slabs(C1_X // LRU_BLOCK) + [
        pl.BlockSpec((1, HALO, LRU_WIDTH), lambda b, i: (b, jnp.maximum(tmap(i) * th - 1, 0), 0)),
        pl.BlockSpec((1, HALO, LRU_WIDTH), lambda b, i: (b, jnp.minimum((tmap(i) + 1) * th, nh - 1), 0)),
        pl.BlockSpec((1, 8, LRU_WIDTH), lambda b, i: (layer, 0, 0)),
        pl.BlockSpec((1, 1, LRU_WIDTH), lambda b, i: (layer, 0, 0)),
        pl.BlockSpec((1, LRU_BLOCKS, LRU_BLOCK, 2 * LRU_BLOCK), lambda b, i: (layer * 2 + d, 0, 0, 0)),
        pl.BlockSpec((1, 1, LRU_WIDTH), lambda b, i: (layer * 2 + d, 0, 0)),
        pl.BlockSpec((1, 1, LRU_WIDTH), lambda b, i: (layer * 2 + d, 0, 0)),
        pl.BlockSpec((1, 1, LRU_WIDTH), lambda b, i: (layer * 2 + d, 0, 0)),
    ]
    args = [cols1] * (LRU_BLOCKS + 2) + [cw, cb, w, ba, bx, lam]
    scratch = [
        pltpu.VMEM((LRU_BLOCKS, tt, LRU_BLOCK), F32),
        pltpu.VMEM((LRU_BLOCKS, tt, LRU_BLOCK), F32),
        pltpu.VMEM((LRU_BLOCKS, 1, LRU_BLOCK), F32),
    ]
    if rev:
        in_specs += [pl.BlockSpec((LRU_BLOCKS, 1, tt, LRU_BLOCK), lambda b, i: (0, b, tmap(i), 0))]
        in_specs += slabs(C1_GC // LRU_BLOCK)
        args += [h_fwd] + [cols1] * LRU_BLOCKS
        scratch += [pltpu.VMEM((LRU_BLOCKS, tt, LRU_BLOCK), F32)]
        out_spec = pl.BlockSpec((1, tt, LRU_WIDTH), lambda b, i: (b, tmap(i), 0))
        out_shape = jax.ShapeDtypeStruct((bsz, lq, LRU_WIDTH), BF16)
    else:
        out_spec = pl.BlockSpec((LRU_BLOCKS, 1, tt, LRU_BLOCK), lambda b, i: (0, b, tmap(i), 0))
        out_shape = jax.ShapeDtypeStruct((LRU_BLOCKS, bsz, lq, LRU_BLOCK), F32)
    return pl.pallas_call(
        functools.partial(_lru_kernel, rev=rev, tt=tt, nt=nt, pad=pad),
        grid=(bsz, nt),
        in_specs=in_specs,
        out_specs=out_spec,
        out_shape=out_shape,
        scratch_shapes=scratch,
        compiler_params=_params(("parallel", "arbitrary")),
        name="lru_bwd" if rev else "lru_fwd",
    )(*args)


def _out_kernel(z_ref, g_ref, mask_ref, ys_ref, ga_ref, yb_ref, yc_ref, wglu_ref, bglu_ref,
                wm0_ref, wm1_ref, wm2_ref, woa_ref, wob_ref, woc_ref, wo_ref, fg_ref,
                o_ref, h_ref, ya_ref, *, final):
    j = pl.program_id(1)
    nj = pl.num_programs(1)

    @pl.when(j == 0)
    def _():
        z = z_ref[...]
        ms = jnp.mean(z * z, axis=-1, keepdims=True)
        h_ref[...] = (z * lax.rsqrt(ms + EPS) * g_ref[0]).astype(BF16)
        zz = jax.nn.gelu(ys_ref[...])
        glu = _sigmoid(_dot(zz.astype(BF16), wglu_ref[0]) + bglu_ref[0])
        ga = ga_ref[...]
        ya_ref[...] = (zz * glu * (ga * _sigmoid(ga))).astype(BF16)

    h = h_ref[...]
    m = (_sigmoid(_dot(h, wm0_ref[0])) * _dot(ya_ref[...], woa_ref[0])
         + _sigmoid(_dot(h, wm1_ref[0])) * _dot(yb_ref[...], wob_ref[0])
         + _sigmoid(_dot(h, wm2_ref[0])) * _dot(yc_ref[...], woc_ref[0]))
    contrib = _dot(m.astype(BF16), wo_ref[0])

    @pl.when(j == 0)
    def _():
        o_ref[...] = contrib

    @pl.when(jnp.logical_and(j > 0, j < nj - 1))
    def _():
        o_ref[...] += contrib

    @pl.when(j == nj - 1)
    def _():
        zn = z_ref[...] + (o_ref[...] + contrib)
        if final:
            ms = jnp.mean(zn * zn, axis=-1, keepdims=True)
            o_ref[...] = zn * lax.rsqrt(ms + EPS) * fg_ref[...]
        else:
            o_ref[...] = jnp.where(mask_ref[...] > 0.0, zn, 0.0)


def _out_proj(z, g, mask, cols1, ys, yb, yc, wglu, bglu, wm, woa, wob, woc, wo, fg, layer, tm, final):
    r, d = z.shape
    TJ = TJ_WIDE if tm <= OUT_TILE else TJ_NARROW
    nj = d // TJ
    return pl.pallas_call(
        functools.partial(_out_kernel, final=final),
        grid=(r // tm, nj),
        in_specs=[
            pl.BlockSpec((tm, d), lambda i, j: (i, 0)),
            pl.BlockSpec((1, 1, d), lambda i, j: (layer, 0, 0)),
            pl.BlockSpec((tm, 1), lambda i, j: (i, 0)),
            pl.BlockSpec((tm, S5_WIDTH), lambda i, j: (i, 0)),
            pl.BlockSpec((tm, S5_WIDTH), lambda i, j: (i, C1_GA // S5_WIDTH)),
            pl.BlockSpec((tm, GLA_WIDTH), lambda i, j: (i, 0)),
            pl.BlockSpec((tm, LRU_WIDTH), lambda i, j: (i, 0)),
            pl.BlockSpec((1, S5_WIDTH, S5_WIDTH), lambda i, j: (layer, 0, 0)),
            pl.BlockSpec((1, 1, S5_WIDTH), lambda i, j: (layer, 0, 0)),
            pl.BlockSpec((1, d, TJ), lambda i, j: (layer, 0, j)),
            pl.BlockSpec((1, d, TJ), lambda i, j: (layer, 0, nj + j)),
            pl.BlockSpec((1, d, TJ), lambda i, j: (layer, 0, 2 * nj + j)),
            pl.BlockSpec((1, S5_WIDTH, TJ), lambda i, j: (layer, 0, j)),
            pl.BlockSpec((1, GLA_WIDTH, TJ), lambda i, j: (layer, 0, j)),
            pl.BlockSpec((1, LRU_WIDTH, TJ), lambda i, j: (layer, 0, j)),
            pl.BlockSpec((1, TJ, d), lambda i, j: (layer, j, 0)),
            pl.BlockSpec((1, d), lambda i, j: (0, 0)),
        ],
        out_specs=pl.BlockSpec((tm, d), lambda i, j: (i, 0)),
        out_shape=jax.ShapeDtypeStruct((r, d), F32),
        scratch_shapes=[
            pltpu.VMEM((tm, d), BF16),
            pltpu.VMEM((tm, S5_WIDTH), BF16),
        ],
        compiler_params=_params(("parallel", "arbitrary")),
        name="out_proj",
    )(z, g, mask, ys, cols1, yb, yc, wglu, bglu, wm, wm, wm, woa, wob, woc, wo, fg)


def _prepare(p):
    depth = p["w_in"].shape[0]
    w_in = p["w_in"]
    o = 0
    seg = {}
    for name, width in (("u", 512), ("ga", 512), ("q", 256), ("k", 256), ("v", 512), ("gb", 512),
                        ("glr", 2 * GLA_RANK), ("x", 1024), ("gc", 1024), ("m", 3 * w_in.shape[1])):
        seg[name] = w_in[:, :, o:o + width]
        o += width
    w1 = jnp.concatenate([seg["x"], seg["gc"], seg["u"], seg["ga"]], axis=-1).astype(BF16)
    zpad = jnp.zeros(w_in.shape[:2] + (N2 - C2_GLR - 2 * GLA_RANK,), w_in.dtype)
    w2 = jnp.concatenate([seg["v"], seg["gb"], seg["q"], seg["k"], seg["glr"], zpad], axis=-1).astype(BF16)
    wm = seg["m"].astype(BF16)

    ng = S5_WIDTH // 128
    eye = jnp.eye(S5_GB, dtype=F32)

    def rows(a):
        return a.astype(F32).reshape(depth, 2, ng, S5_NS)

    ls = jnp.broadcast_to(p["s5_log_step"].astype(F32)[..., None], p["s5_lam_re"].shape)
    lr, li, ls = rows(p["s5_lam_re"]), rows(p["s5_lam_im"]), rows(ls)
    zero = jnp.zeros_like(lr[:, 0])
    prm = jnp.stack([lr[:, 0], li[:, 0], ls[:, 0], lr[:, 1], li[:, 1], ls[:, 1], zero, zero], axis=2)

    def emb_b(bm):
        bm = bm.astype(F32).reshape(depth, ng, S5_GB, 64, 16)
        return jnp.einsum("lGgnc,gh->lGgchn", bm, eye).reshape(depth, ng, 128, S5_NS)

    def emb_c(cm):
        cm = cm.astype(F32).reshape(depth, 2, ng, S5_GB, 16, 64)
        return jnp.einsum("ldGgcn,gh->ldGgchn", cm, eye).reshape(depth, 2, ng, 128, S5_NS)

    wg = jnp.zeros((depth, 2, 128, GLA_KEY), F32)
    for d in range(2):
        wg = wg.at[:, d, d * GLA_RANK:(d + 1) * GLA_RANK, :].set(p["gla_w_gate_up"][:, d].astype(F32))
    lru_w = (0.5 * jnp.concatenate([p["lru_w_a"], p["lru_w_x"]], axis=-1)).astype(BF16)
    cw = jnp.concatenate([p["conv_w"].astype(F32),
                          jnp.zeros((depth, 8 - CONV_WIDTH, LRU_WIDTH), F32)], axis=1)
    return dict(
        norm_g=p["norm_g"].astype(F32).reshape(depth, 1, -1), w1=w1, w2=w2, wm=wm,
        s5_prm=prm, s5_lbr=emb_b(p["s5_b_re"]), s5_lbi=emb_b(p["s5_b_im"]),
        s5_ctr=emb_c(p["s5_c_re"]), s5_cti=emb_c(p["s5_c_im"]),
        s5_d=p["s5_d"].astype(F32).reshape(depth, ng, 1, 128),
        wglu=p["s5_w_glu"].astype(BF16), bglu=p["s5_b_glu"].astype(F32).reshape(depth, 1, S5_WIDTH),
        wg=wg.reshape(depth * 2, 128, GLA_KEY).astype(BF16),
        bg=p["gla_b_gate"].astype(F32).reshape(depth * 2, 1, GLA_KEY),
        gla_ng=p["gla_norm_g"].astype(F32).reshape(depth, 1, GLA_WIDTH),
        cw=cw, cb=p["conv_b"].astype(F32).reshape(depth, 1, LRU_WIDTH),
        lru_w=lru_w.reshape(depth * 2, LRU_BLOCKS, LRU_BLOCK, 2 * LRU_BLOCK),
        lru_ba=p["lru_b_a"].astype(F32).reshape(depth * 2, 1, LRU_WIDTH),
        lru_bx=p["lru_b_x"].astype(F32).reshape(depth * 2, 1, LRU_WIDTH),
        lru_lam=p["lru_lam"].astype(F32).reshape(depth * 2, 1, LRU_WIDTH),
        woa=p["w_out_a"].astype(BF16), wob=p["w_out_b"].astype(BF16), woc=p["w_out_c"].astype(BF16),
        wo=p["w_o"].astype(BF16), fg=p["final_norm_g"].astype(F32).reshape(1, -1),
    )


def _pick_tile(n, unit, target):
    best = unit
    for t in range(unit, target + 1, unit):
        if n % t == 0:
            best = t
    return best


def _pick_lru_tile(lq, target):
    best = None
    for seg in range(1, target // 8 + 1):
        if lq % (8 * seg) == 0 and seg % 8 != 0:
            best = 8 * seg
    assert best is not None
    return best


def _encoder(x, meta, w):
    bsz, seq, d = x.shape
    depth = w["w2"].shape[0]
    pad = PAD
    while (bsz * (pad + N_META + seq)) % OUT_TILE != 0 and pad < PAD + 4 * GLA_CHUNK:
        pad += GLA_CHUNK
    if (bsz * (pad + N_META + seq)) % OUT_TILE != 0:
        pad = PAD
    lq = pad + N_META + seq
    r = bsz * lq
    head = jnp.concatenate([jnp.zeros((pad, d), F32), meta.astype(F32)], axis=0)
    z = jnp.concatenate([jnp.broadcast_to(head[None], (bsz, pad + N_META, d)), x.astype(F32)], axis=1)
    z = z.reshape(r, d)
    mask = jnp.broadcast_to((jnp.arange(lq) >= pad).astype(F32)[None, :, None], (bsz, lq, 1)).reshape(r, 1)
    tm = _pick_tile(r, LANES, IN_TILE_MAX)
    tm_out = _pick_tile(r, LANES, OUT_TILE_MAX)
    cpt = _pick_tile(lq // GLA_CHUNK, 1, GLA_CHUNKS_MAX)
    tt = _pick_lru_tile(lq, LRU_TILE_MAX)
    for l in range(depth):
        cols1, cols2 = _in_proj(z, w["norm_g"], w["w1"], w["w2"], l, tm)
        c1 = cols1.reshape(bsz, lq, N1)
        c2 = cols2.reshape(bsz, lq, N2)
        ys = _s5(c1, w["s5_prm"], w["s5_lbr"], w["s5_lbi"], w["s5_ctr"], w["s5_cti"], w["s5_d"], l)
        of = _gla_pass(c2, w["wg"], w["bg"], l, False, cpt, pad)
        yb = _gla_pass(c2, w["wg"], w["bg"], l, True, cpt, pad, of, w["gla_ng"])
        lru_w = (w["cw"], w["cb"], w["lru_w"], w["lru_ba"], w["lru_bx"], w["lru_lam"])
        hf = _lru_pass(c1, *lru_w, l, False, tt, pad)
        yc = _lru_pass(c1, *lru_w, l, True, tt, pad, hf)
        z = _out_proj(z, w["norm_g"], mask, cols1, ys.reshape(r, -1), yb.reshape(r, -1),
                      yc.reshape(r, -1), w["wglu"], w["bglu"], w["wm"], w["woa"], w["wob"], w["woc"],
                      w["wo"], w["fg"], l, tm_out, l == depth - 1)
    return z.reshape(bsz, lq, d)[:, pad + N_META:].astype(x.dtype)


def kernel(x_prompt, x_sample, meta_tokens, norm_g, w_in, s5_lam_re, s5_lam_im, s5_log_step, s5_b_re, s5_b_im, s5_c_re, s5_c_im, s5_d, s5_w_glu, s5_b_glu, gla_w_gate_up, gla_b_gate, gla_norm_g, conv_w, conv_b, lru_w_a, lru_b_a, lru_w_x, lru_b_x, lru_lam, w_out_a, w_out_b, w_out_c, w_o, final_norm_g):
    w = _prepare(dict(
        norm_g=norm_g, w_in=w_in, s5_lam_re=s5_lam_re, s5_lam_im=s5_lam_im, s5_log_step=s5_log_step,
        s5_b_re=s5_b_re, s5_b_im=s5_b_im, s5_c_re=s5_c_re, s5_c_im=s5_c_im, s5_d=s5_d,
        s5_w_glu=s5_w_glu, s5_b_glu=s5_b_glu, gla_w_gate_up=gla_w_gate_up, gla_b_gate=gla_b_gate,
        gla_norm_g=gla_norm_g, conv_w=conv_w, conv_b=conv_b, lru_w_a=lru_w_a, lru_b_a=lru_b_a,
        lru_w_x=lru_w_x, lru_b_x=lru_b_x, lru_lam=lru_lam, w_out_a=w_out_a, w_out_b=w_out_b,
        w_out_c=w_out_c, w_o=w_o, final_norm_g=final_norm_g))
    return (_encoder(x_prompt, meta_tokens, w), _encoder(x_sample, meta_tokens, w))
```
